```python
import math
import jax, jax.numpy as jnp
from jax import lax
import numpy as np

D_MODEL = 1024
BATCH = 4
SEQ = 4096
DEPTH = 1

N_META = 16
RWKV_HEADS = 16
RWKV_HEAD_DIM = 64
RWKV_WIDTH = RWKV_HEADS * RWKV_HEAD_DIM
LORA_W = 64
LORA_A = 64
SB_HEADS = 16
SB_HEAD_DIM = 64
SB_WIDTH = SB_HEADS * SB_HEAD_DIM
SB_BLOCK = 128
N_BRANCH = 2
RWKV_SHIFT_COLS = 3 * RWKV_WIDTH + LORA_W + LORA_A
IN_SIZES = (RWKV_SHIFT_COLS, RWKV_WIDTH, SB_WIDTH, SB_WIDTH, SB_WIDTH, SB_WIDTH, N_BRANCH * D_MODEL)
IN_COLS = RWKV_SHIFT_COLS + RWKV_WIDTH + 4 * SB_WIDTH + N_BRANCH * D_MODEL
RMS_EPS = 1e-6
GN_EPS = 64e-5
L2_EPS = 1e-12
DECAY_SCALE = 0.6065306597126334

kernel_name = "hybrid_rwkv7_stickbreaking_gated_layer"


def _split(t, sizes):
    outs = []
    start = 0
    for size in sizes:
        outs.append(t[..., start:start + size])
        start += size
    return outs


def _rms_norm(t, g):
    t32 = t.astype(jnp.float32)
    inv = lax.rsqrt(jnp.mean(t32 * t32, axis=-1, keepdims=True) + RMS_EPS)
    return (t32 * inv * g.astype(jnp.float32)).astype(t.dtype)


def _heads(t, n_heads):
    return t.reshape(t.shape[:-1] + (n_heads, t.shape[-1] // n_heads))


def _rwkv7_step(state, inputs):
    r, decay, kk, a, k, v = inputs
    sa = jnp.einsum("bhvk,bhk->bhv", state, kk)
    state = (state * decay[:, :, None, :]
             - sa[..., None] * (kk * a)[:, :, None, :]
             + v[..., None] * k[:, :, None, :])
    y = jnp.einsum("bhvk,bhk->bhv", state, r)
    return state, y


def _rwkv7_time_mix(u, mu, w0, w_up, a0, a_up, k_k, k_a, r_k, gn_g, gn_b):
    bsz, length, _ = u.shape
    f32 = jnp.float32
    u_prev = jnp.pad(u, ((0, 0), (1, 0), (0, 0)))[:, :-1]
    u = u + (u_prev - u) * mu
    r, k, v, w_lo, a_lo = _split(u, (RWKV_WIDTH, RWKV_WIDTH, RWKV_WIDTH, LORA_W, LORA_A))
    decay = jnp.exp(-DECAY_SCALE * jax.nn.sigmoid((w0 + jnp.tanh(w_lo) @ w_up).astype(f32)))
    a = jax.nn.sigmoid((a0 + a_lo @ a_up).astype(f32))
    k = k.astype(f32)
    kk = _heads(k * k_k, RWKV_HEADS)
    kk = kk * lax.rsqrt(jnp.sum(kk * kk, axis=-1, keepdims=True) + L2_EPS)
    k = _heads(k * (1.0 + (a - 1.0) * k_a), RWKV_HEADS)
    r = _heads(r.astype(f32), RWKV_HEADS)
    v = _heads(v.astype(f32), RWKV_HEADS)
    decay = _heads(decay, RWKV_HEADS)
    a = _heads(a, RWKV_HEADS)
    state0 = jnp.zeros((bsz, RWKV_HEADS, RWKV_HEAD_DIM, RWKV_HEAD_DIM), f32)
    xs = tuple(jnp.moveaxis(t, 1, 0) for t in (r, decay, kk, a, k, v))
    _, y = lax.scan(_rwkv7_step, state0, xs)
    y = jnp.moveaxis(y, 0, 1)
    mean = jnp.mean(y, axis=-1, keepdims=True)
    var = jnp.mean(jnp.square(y - mean), axis=-1, keepdims=True)
    y = ((y - mean) * lax.rsqrt(var + GN_EPS)).reshape(bsz, length, RWKV_WIDTH) * gn_g + gn_b
    bonus = jnp.sum(r * k * r_k, axis=-1, keepdims=True) * v
    return (y + bonus.reshape(bsz, length, RWKV_WIDTH)).astype(u.dtype)


def _stick_breaking_attention(q, k, v):
    length = q.shape[2]
    scale = SB_HEAD_DIM ** -0.5
    starts = [0] + list(range(N_META, length, SB_BLOCK))
    ends = starts[1:] + [length]
    outs = []
    for qs, qe in zip(starts, ends):
        logits = jnp.einsum("bhqd,bhkd->bhqk", q[:, :, qs:qe], k[:, :, :qe]).astype(jnp.float32) * scale
        visible = jnp.arange(qe)[None, :] < jnp.arange(qs, qe)[:, None]
        log_keep = jnp.where(visible, jax.nn.log_sigmoid(-logits), 0.0)
        log_between = lax.cumsum(log_keep, axis=3, reverse=True) - log_keep
        weights = jnp.where(visible, jnp.exp(jax.nn.log_sigmoid(logits) + log_between), 0.0)
        outs.append(jnp.einsum("bhqk,bhkd->bhqd", weights.astype(v.dtype), v[:, :, :qe]))
    return jnp.concatenate(outs, axis=2)


def setup_inputs(seed: int = 0) -> dict:
    key = jax.random.key(seed)
    ks = jax.random.split(key, 18)
    f32 = jnp.float32

    def nrm(k, shape, s):
        return jax.random.normal(k, shape, f32) * s

    return {
        "x": nrm(ks[0], (BATCH, SEQ, D_MODEL), 1.0),
        "meta_tokens": nrm(ks[1], (N_META, D_MODEL), 1.0),
        "pre_norm_g": 1.0 + nrm(ks[2], (DEPTH, D_MODEL), 0.05),
        "post_norm_g": 1.0 + nrm(ks[3], (DEPTH, D_MODEL), 0.05),
        "w_in": nrm(ks[4], (DEPTH, D_MODEL, IN_COLS), D_MODEL ** -0.5),
        "rwkv_mu": jax.random.uniform(ks[5], (DEPTH, RWKV_SHIFT_COLS), f32),
        "rwkv_w0": nrm(ks[6], (DEPTH, RWKV_WIDTH), 0.5),
        "rwkv_w_up": nrm(ks[7], (DEPTH, LORA_W, RWKV_WIDTH), 0.5 * LORA_W ** -0.5),
        "rwkv_a0": nrm(ks[8], (DEPTH, RWKV_WIDTH), 0.5),
        "rwkv_a_up": nrm(ks[9], (DEPTH, LORA_A, RWKV_WIDTH), 0.5 * LORA_A ** -0.5),
        "rwkv_k_k": 0.85 + nrm(ks[10], (DEPTH, RWKV_WIDTH), 0.05),
        "rwkv_k_a": 1.0 + nrm(ks[11], (DEPTH, RWKV_WIDTH), 0.05),
        "rwkv_r_k": nrm(ks[12], (DEPTH, RWKV_HEADS, RWKV_HEAD_DIM), 0.1),
        "rwkv_gn_g": 1.0 + nrm(ks[13], (DEPTH, RWKV_WIDTH), 0.05),
        "rwkv_gn_b": nrm(ks[14], (DEPTH, RWKV_WIDTH), 0.02),
        "w_proj_rwkv": nrm(ks[15], (DEPTH, RWKV_WIDTH, D_MODEL), RWKV_WIDTH ** -0.5),
        "w_proj_sb": nrm(ks[16], (DEPTH, SB_WIDTH, D_MODEL), SB_WIDTH ** -0.5),
        "w_out": nrm(ks[17], (DEPTH, D_MODEL, D_MODEL), D_MODEL ** -0.5),
    }


def reference(x, meta_tokens, pre_norm_g, post_norm_g, w_in, rwkv_mu, rwkv_w0, rwkv_w_up,
              rwkv_a0, rwkv_a_up, rwkv_k_k, rwkv_k_a, rwkv_r_k, rwkv_gn_g, rwkv_gn_b,
              w_proj_rwkv, w_proj_sb, w_out):
    bsz = x.shape[0]
    meta = jnp.broadcast_to(meta_tokens[None].astype(x.dtype), (bsz, N_META, D_MODEL))
    h = jnp.concatenate([meta, x], axis=1)
    length = h.shape[1]
    for l in range(DEPTH):
        hn = _rms_norm(h, pre_norm_g[l])
        proj = hn @ w_in[l]
        u_rwkv, g_rwkv, q_sb, k_sb, v_sb, g_sb, merge_logits = _split(proj, IN_SIZES)

        y_rwkv = _rwkv7_time_mix(u_rwkv, rwkv_mu[l], rwkv_w0[l], rwkv_w_up[l], rwkv_a0[l],
                                 rwkv_a_up[l], rwkv_k_k[l], rwkv_k_a[l], rwkv_r_k[l],
                                 rwkv_gn_g[l], rwkv_gn_b[l]) * jax.nn.silu(g_rwkv)

        to_bhld = lambda t: jnp.transpose(_heads(t, SB_HEADS), (0, 2, 1, 3))
        o_sb = _stick_breaking_attention(to_bhld(q_sb), to_bhld(k_sb), to_bhld(v_sb))
        y_sb = jnp.transpose(o_sb, (0, 2, 1, 3)).reshape(bsz, length, SB_WIDTH) * jax.nn.silu(g_sb)

        gate_rwkv, gate_sb = _split(jax.nn.sigmoid(merge_logits), (D_MODEL, D_MODEL))
        mixed = gate_rwkv * (y_rwkv @ w_proj_rwkv[l]) + gate_sb * (y_sb @ w_proj_sb[l])
        h = h + _rms_norm(mixed @ w_out[l], post_norm_g[l])
    return h[:, N_META:]
```

```python
import functools

import jax
import jax.numpy as jnp
from jax import lax
from jax.experimental import pallas as pl
from jax.experimental.pallas import tpu as pltpu

F32 = jnp.float32
BF16 = jnp.bfloat16

D_MODEL = 1024
N_META = 16
HEADS = 16
HEAD_DIM = 64
WIDTH = HEADS * HEAD_DIM
LORA = 64
RMS_EPS = 1e-6
GN_EPS = 64e-5
L2_EPS = 1e-12
DECAY_SCALE = 0.6065306597126334

LANES = 128
PAIRS = WIDTH // LANES
CHUNK = 64
N_SEG = 10
SEG_R, SEG_K, SEG_V, SEG_GR, SEG_Q, SEG_KS, SEG_VS, SEG_GS, SEG_MR, SEG_MS = range(N_SEG)
VMEM_LIMIT = 56 * 1024 * 1024


def _dot(a, b):
    return jnp.dot(a.astype(BF16), b.astype(BF16), preferred_element_type=F32)


def _dot_nt(a, b):
    return lax.dot_general(a.astype(BF16), b.astype(BF16), (((1,), (1,)), ((), ())),
                           preferred_element_type=F32)


def _sigmoid(x):
    return 1.0 / (1.0 + jnp.exp(-x))


def _in_proj_kernel(x_ref, g_ref, w_ref, wl_ref, p_ref, lora_ref, hn_ref):
    @pl.when(pl.program_id(1) == 0)
    def _():
        x = x_ref[...]
        inv = lax.rsqrt(jnp.mean(x * x, axis=-1, keepdims=True) + RMS_EPS)
        hn = (x * inv * g_ref[...]).astype(BF16)
        hn_ref[...] = hn
        lora_ref[...] = jnp.dot(hn, wl_ref[...], preferred_element_type=F32)

    p_ref[...] = jnp.dot(hn_ref[...], w_ref[...], preferred_element_type=F32).astype(p_ref.dtype)


def _in_proj(x2d, g, w_main, w_lora, tm, tn):
    rows = x2d.shape[0]
    cols = w_main.shape[1]
    return pl.pallas_call(
        _in_proj_kernel,
        grid=(rows // tm, cols // tn),
        in_specs=[
            pl.BlockSpec((tm, D_MODEL), lambda i, j: (i, 0)),
            pl.BlockSpec((1, D_MODEL), lambda i, j: (0, 0)),
            pl.BlockSpec((D_MODEL, tn), lambda i, j: (0, j)),
            pl.BlockSpec((D_MODEL, 2 * LORA), lambda i, j: (0, 0)),
        ],
        out_specs=[
            pl.BlockSpec((tm, tn), lambda i, j: (i, j)),
            pl.BlockSpec((tm, 2 * LORA), lambda i, j: (i, 0)),
        ],
        out_shape=[
            jax.ShapeDtypeStruct((rows, cols), BF16),
            jax.ShapeDtypeStruct((rows, 2 * LORA), F32),
        ],
        scratch_shapes=[pltpu.VMEM((tm, D_MODEL), BF16)],
        compiler_params=pltpu.CompilerParams(
            dimension_semantics=("arbitrary", "arbitrary"), vmem_limit_bytes=VMEM_LIMIT),
        name="in_proj",
    )(x2d, g, w_main, w_lora)


def _head_sum(x, e256):
    groups = WIDTH // 256
    stacked = jnp.concatenate([x[:, 256 * g:256 * (g + 1)] for g in range(groups)], axis=0)
    s = _dot(stacked, e256)
    return jnp.concatenate([s[CHUNK * g:CHUNK * (g + 1), :] for g in range(groups)], axis=1)


def _rwkv_kernel(r_ref, k_ref, v_ref, gr_ref, lo_ref, pr_ref, pk_ref, pv_ref, plo_ref,
                 mu_ref, mul_ref, w0_ref, a0_ref, kk_ref, ka_ref, rk_ref, gng_ref, gnb_ref,
                 wl_ref, e256_ref, o_ref, s_ref, last_ref, lastl_ref):
    c = pl.program_id(1)
    is_meta = c == 0

    @pl.when(is_meta)
    def _():
        s_ref[...] = jnp.zeros_like(s_ref)
        last_ref[...] = jnp.zeros_like(last_ref)
        lastl_ref[...] = jnp.zeros_like(lastl_ref)

    def pick(meta_ref, main_ref):
        return jnp.where(is_meta, meta_ref[...].astype(F32), main_ref[...].astype(F32))

    row = lax.broadcasted_iota(jnp.int32, (CHUNK, 1), 0)

    def shift(cur, last_row, mu):
        prev = jnp.where(row == 0, last_row, pltpu.roll(cur, 1, 0))
        return cur + (prev - cur) * mu

    r_raw = pick(pr_ref, r_ref)
    k_raw = pick(pk_ref, k_ref)
    v_raw = pick(pv_ref, v_ref)
    lo_raw = pick(plo_ref, lo_ref)
    r = shift(r_raw, last_ref[0:1, :], mu_ref[0:1, :])
    k = shift(k_raw, last_ref[1:2, :], mu_ref[1:2, :])
    v = shift(v_raw, last_ref[2:3, :], mu_ref[2:3, :])
    lo = shift(lo_raw, lastl_ref[0:1, :], mul_ref[...])
    last_ref[0:1, :] = r_raw[CHUNK - 1:CHUNK, :]
    last_ref[1:2, :] = k_raw[CHUNK - 1:CHUNK, :]
    last_ref[2:3, :] = v_raw[CHUNK - 1:CHUNK, :]
    lastl_ref[0:1, :] = lo_raw[CHUNK - 1:CHUNK, :]

    lane_l = lax.broadcasted_iota(jnp.int32, (CHUNK, 2 * LORA), 1)
    pre_w = _dot(jnp.where(lane_l < LORA, jnp.tanh(lo), 0.0), wl_ref[...])
    pre_a = _dot(jnp.where(lane_l >= LORA, lo, 0.0), wl_ref[...])
    logw = -DECAY_SCALE * _sigmoid(w0_ref[...] + pre_w)
    a = _sigmoid(a0_ref[...] + pre_a)

    ti = lax.broadcasted_iota(jnp.int32, (CHUNK, CHUNK), 0)
    si = lax.broadcasted_iota(jnp.int32, (CHUNK, CHUNK), 1)
    tri = (si <= ti).astype(BF16)
    lw_hi = logw.astype(BF16)
    lw_lo = (logw - lw_hi.astype(F32)).astype(BF16)
    cw = (jnp.dot(tri, lw_hi, preferred_element_type=F32)
          + jnp.dot(tri, lw_lo, preferred_element_type=F32))
    w_in = jnp.exp(cw)
    w_ex = jnp.exp(cw - logw)
    w_inv = jnp.exp(-cw)
    w_end = w_in[CHUNK - 1:CHUNK, :]

    e256 = e256_ref[...]
    kk = k * kk_ref[...]
    kappa = kk * lax.rsqrt(_head_sum(kk * kk, e256) + L2_EPS)
    k_rep = k * (1.0 + (a - 1.0) * ka_ref[...])
    kappa_h = kappa * w_ex
    r_h = r * w_in
    k_h = k_rep * w_inv
    b_h = kappa * a * w_inv

    lane = lax.broadcasted_iota(jnp.int32, (CHUNK, LANES), 1)
    first = lane < HEAD_DIM
    col = jnp.where(first, lane, lane - HEAD_DIM)
    strict = col < row
    incl = col <= row
    row2 = lax.broadcasted_iota(jnp.int32, (2 * CHUNK, LANES), 0)
    lane2 = lax.broadcasted_iota(jnp.int32, (2 * CHUNK, LANES), 1)
    same_head = (row2 < HEAD_DIM) == (lane2 < HEAD_DIM)
    eye2 = (row2 == lane2).astype(F32)

    def blockdiag(t):
        return jnp.concatenate([jnp.where(first, t, 0.0), jnp.where(first, 0.0, t)], axis=0)

    ys = []
    for p in range(PAIRS):
        sl = slice(LANES * p, LANES * (p + 1))
        kap_p, r_p, k_p, b_p, v_p = kappa_h[:, sl], r_h[:, sl], k_h[:, sl], b_h[:, sl], v[:, sl]
        s_bd = s_ref[p]
        lhs = jnp.concatenate([kap_p, r_p], axis=0).astype(BF16)
        k_bd = blockdiag(k_p)
        b_bd = blockdiag(b_p)
        gram = _dot_nt(lhs, jnp.concatenate([k_bd, b_bd], axis=0))
        sp = _dot_nt(lhs, s_bd)
        mk_kap = jnp.where(strict, gram[:CHUNK, :LANES], 0.0)
        mb_kap = jnp.where(strict, gram[:CHUNK, LANES:], 0.0)
        mk_r = jnp.where(incl, gram[CHUNK:, :LANES], 0.0)
        mb_r = jnp.where(incl, gram[CHUNK:, LANES:], 0.0)
        mv = _dot(jnp.concatenate([mk_kap, mk_r], axis=0), blockdiag(v_p))
        rhs = sp[:CHUNK] + mv[:CHUNK]

        l_bd = blockdiag(mb_kap)
        t_inv = eye2
        for level in range(CHUNK.bit_length() - 1):
            rb = jnp.right_shift(row2, level)
            off = ((rb & 1) == 1) & (jnp.right_shift(lane2, level) == rb - 1)
            l_off = jnp.where(off, l_bd, 0.0)
            if level == 0:
                t_inv = eye2 - l_off
            else:
                t_inv = t_inv - _dot(t_inv, _dot(l_off, t_inv))

        u_bd = _dot(t_inv, blockdiag(rhs))
        u_p = u_bd[:CHUNK] + u_bd[CHUNK:]
        ys.append(sp[CHUNK:] + mv[CHUNK:] - _dot(mb_r, u_bd))
        vu_t = jnp.concatenate([v_p, -u_p], axis=0).T
        ds = _dot(vu_t, jnp.concatenate([k_p, b_p], axis=0))
        s_ref[p] = (s_bd + jnp.where(same_head, ds, 0.0)) * w_end[:, sl]

    y = jnp.concatenate(ys, axis=1)
    mean = _head_sum(y, e256) * (1.0 / HEAD_DIM)
    d = y - mean
    var = _head_sum(d * d, e256) * (1.0 / HEAD_DIM)
    y = d * lax.rsqrt(var + GN_EPS) * gng_ref[...] + gnb_ref[...]
    bonus = _head_sum(r * k_rep * rk_ref[...], e256) * v
    g = jnp.where(is_meta, 0.0, gr_ref[...].astype(F32))
    o_ref[...] = ((y + bonus) * (g * _sigmoid(g))).astype(o_ref.dtype)


def _rwkv(p_main, lora_main, p_meta, lora_meta, rows_vec, mu_lora, w_lora, e256, batch, seq):
    n_chunks = seq // CHUNK
    per_b = seq // CHUNK

    def main_map(seg):
        return lambda b, c: (b * per_b + jnp.maximum(c - 1, 0), seg)

    def meta_map(seg):
        return lambda b, c: (0, seg)

    row_spec = pl.BlockSpec((1, WIDTH), lambda b, c: (0, 0))
    in_specs = [
        pl.BlockSpec((CHUNK, WIDTH), main_map(SEG_R)),
        pl.BlockSpec((CHUNK, WIDTH), main_map(SEG_K)),
        pl.BlockSpec((CHUNK, WIDTH), main_map(SEG_V)),
        pl.BlockSpec((CHUNK, WIDTH), main_map(SEG_GR)),
        pl.BlockSpec((CHUNK, 2 * LORA), main_map(0)),
        pl.BlockSpec((CHUNK, WIDTH), meta_map(SEG_R)),
        pl.BlockSpec((CHUNK, WIDTH), meta_map(SEG_K)),
        pl.BlockSpec((CHUNK, WIDTH), meta_map(SEG_V)),
        pl.BlockSpec((CHUNK, 2 * LORA), meta_map(0)),
        pl.BlockSpec((3, WIDTH), lambda b, c: (0, 0)),
        pl.BlockSpec((1, 2 * LORA), lambda b, c: (0, 0)),
    ] + [row_spec] * 7 + [
        pl.BlockSpec((2 * LORA, WIDTH), lambda b, c: (0, 0)),
        pl.BlockSpec((256, 256), lambda b, c: (0, 0)),
    ]
    mu3, w0, a0, k_k, k_a, r_k, gn_g, gn_b = rows_vec
    return pl.pallas_call(
        _rwkv_kernel,
        grid=(batch, n_chunks + 1),
        in_specs=in_specs,
        out_specs=pl.BlockSpec((CHUNK, WIDTH), main_map(0)),
        out_shape=jax.ShapeDtypeStruct((batch * seq, WIDTH), BF16),
        scratch_shapes=[
            pltpu.VMEM((PAIRS, LANES, LANES), F32),
            pltpu.VMEM((8, WIDTH), F32),
            pltpu.VMEM((8, 2 * LORA), F32),
        ],
        compiler_params=pltpu.CompilerParams(
            dimension_semantics=("arbitrary", "arbitrary"), vmem_limit_bytes=VMEM_LIMIT),
        name="rwkv",
    )(p_main, p_main, p_main, p_main, lora_main, p_meta, p_meta, p_meta, lora_meta,
      mu3, mu_lora, w0, a0, k_k, k_a, r_k, gn_g, gn_b, w_lora, e256)


def _sb_kernel(q_ref, k_ref, v_ref, km_ref, vm_ref, g_ref, uo_ref, o_ref, *, tq, tk):
    i = pl.program_id(2)
    lane_q = lax.broadcasted_iota(jnp.int32, (tq, LANES), 1)
    q = q_ref[...].astype(F32) * (HEAD_DIM ** -0.5)
    qs = jnp.concatenate([jnp.where(lane_q < HEAD_DIM, q, 0.0),
                          jnp.where(lane_q < HEAD_DIM, 0.0, q)], axis=0).astype(BF16)
    uo = uo_ref[...]
    row = lax.broadcasted_iota(jnp.int32, (2 * tq, tk), 0)
    qpos = i * tq + jnp.where(row < tq, row, row - tq)
    lane_k = lax.broadcasted_iota(jnp.int32, (2 * tq, tk), 1)
    lane_v = lax.broadcasted_iota(jnp.int32, (tk, LANES), 1)

    def block(kb, vb, visible, carry, acc):
        z = _dot_nt(qs, kb)
        soft = jnp.log(1.0 + jnp.exp(jnp.minimum(z, -z)))
        lk = jnp.where(visible, jnp.minimum(-z, 0.0) - soft, 0.0)
        cs = jnp.dot(lk.astype(BF16), uo, preferred_element_type=F32)
        w = jnp.where(visible, jnp.exp(z + lk + cs[:, :tk] + carry), 0.0)
        wcat = jnp.concatenate([w[:tq], w[tq:]], axis=1).astype(BF16)
        vbd = jnp.concatenate([jnp.where(lane_v < HEAD_DIM, vb, 0),
                               jnp.where(lane_v < HEAD_DIM, 0, vb)], axis=0)
        return carry + cs[:, tk:], acc + jnp.dot(wcat, vbd, preferred_element_type=F32)

    n_blk = (i + 1) * (tq // tk)

    def body(j, state):
        carry, acc = state
        kbi = n_blk - 1 - j
        start = pl.multiple_of(kbi * tk, tk)
        kb = k_ref[pl.ds(start, tk), :]
        vb = v_ref[pl.ds(start, tk), :]
        visible = (kbi * tk + lane_k) < qpos
        return block(kb, vb, visible, carry, acc)

    carry0 = jnp.zeros((2 * tq, tk), F32)
    acc0 = jnp.zeros((tq, LANES), F32)
    carry, acc = lax.fori_loop(0, n_blk, body, (carry0, acc0))
    _, acc = block(km_ref[...], vm_ref[...], lane_k < N_META, carry, acc)
    g = g_ref[...].astype(F32)
    o_ref[...] = (acc * (g * _sigmoid(g))).astype(o_ref.dtype)


def _sb_attn(p_main, k_meta, v_meta, uo, batch, seq, tq, tk):
    nq = seq // tq

    def seq_map(seg):
        return lambda b, p, i: (b, seg * PAIRS + p)

    def tile_map(seg):
        return lambda b, p, i: (b * nq + i, seg * PAIRS + p)

    return pl.pallas_call(
        functools.partial(_sb_kernel, tq=tq, tk=tk),
        grid=(batch, PAIRS, nq),
        in_specs=[
            pl.BlockSpec((tq, LANES), tile_map(SEG_Q)),
            pl.BlockSpec((seq, LANES), seq_map(SEG_KS)),
            pl.BlockSpec((seq, LANES), seq_map(SEG_VS)),
            pl.BlockSpec((tk, LANES), lambda b, p, i: (0, p)),
            pl.BlockSpec((tk, LANES), lambda b, p, i: (0, p)),
            pl.BlockSpec((tq, LANES), tile_map(SEG_GS)),
            pl.BlockSpec((tk, 2 * tk), lambda b, p, i: (0, 0)),
        ],
        out_specs=pl.BlockSpec((tq, LANES), lambda b, p, i: (b * nq + i, p)),
        out_shape=jax.ShapeDtypeStruct((batch * seq, WIDTH), BF16),
        compiler_params=pltpu.CompilerParams(
            dimension_semantics=("arbitrary", "arbitrary", "arbitrary"),
            vmem_limit_bytes=VMEM_LIMIT),
        name="sb_attn",
    )(p_main, p_main, p_main, k_meta, v_meta, p_main, uo)


def _out_kernel(yr_ref, ys_ref, mr_ref, ms_ref, x_ref, wr_ref, ws_ref, wo_ref, g_ref, o_ref):
    pr = jnp.dot(yr_ref[...], wr_ref[...], preferred_element_type=F32)
    ps = jnp.dot(ys_ref[...], ws_ref[...], preferred_element_type=F32)
    mixed = (_sigmoid(mr_ref[...].astype(F32)) * pr + _sigmoid(ms_ref[...].astype(F32)) * ps)
    o = jnp.dot(mixed.astype(BF16), wo_ref[...], preferred_element_type=F32)
    inv = lax.rsqrt(jnp.mean(o * o, axis=-1, keepdims=True) + RMS_EPS)
    o_ref[...] = x_ref[...] + o * inv * g_ref[...]


def _out_proj(y_rwkv, y_sb, p_main, x2d, w_r, w_s, w_o, g, tm):
    rows = x2d.shape[0]
    w_spec = pl.BlockSpec((WIDTH, D_MODEL), lambda i: (0, 0))
    return pl.pallas_call(
        _out_kernel,
        grid=(rows // tm,),
        in_specs=[
            pl.BlockSpec((tm, WIDTH), lambda i: (i, 0)),
            pl.BlockSpec((tm, WIDTH), lambda i: (i, 0)),
            pl.BlockSpec((tm, WIDTH), lambda i: (i, SEG_MR)),
            pl.BlockSpec((tm, WIDTH), lambda i: (i, SEG_MS)),
            pl.BlockSpec((tm, D_MODEL), lambda i: (i, 0)),
            w_spec, w_spec, w_spec,
            pl.BlockSpec((1, D_MODEL), lambda i: (0, 0)),
        ],
        out_specs=pl.BlockSpec((tm, D_MODEL), lambda i: (i, 0)),
        out_shape=jax.ShapeDtypeStruct((rows, D_MODEL), F32),
        compiler_params=pltpu.CompilerParams(
            dimension_semantics=("arbitrary",), vmem_limit_bytes=VMEM_LIMIT),
        name="out_proj",
    )(y_rwkv, y_sb, p_main, p_main, x2d, w_r, w_s, w_o, g)


def _tiles(batch, seq):
    rows = batch * seq
    tm_in = min(1024, rows)
    tm_out = min(512, rows)
    tq = min(256, seq)
    return tm_in, 2048, tm_out, tq, 128


def kernel(x, meta_tokens, pre_norm_g, post_norm_g, w_in, rwkv_mu, rwkv_w0, rwkv_w_up, rwkv_a0,
           rwkv_a_up, rwkv_k_k, rwkv_k_a, rwkv_r_k, rwkv_gn_g, rwkv_gn_b, w_proj_rwkv, w_proj_sb,
           w_out):
    batch, seq, d_model = x.shape
    assert d_model == D_MODEL and w_in.shape[0] == 1, "single-layer kernel"
    assert seq % 256 == 0 and meta_tokens.shape == (N_META, D_MODEL)
    tm_in, tn_in, tm_out, tq, tk = _tiles(batch, seq)

    w = w_in[0]
    w_main = jnp.concatenate([w[:, :3 * WIDTH], w[:, 3 * WIDTH + 2 * LORA:]], axis=1).astype(BF16)
    w_lo = w[:, 3 * WIDTH:3 * WIDTH + 2 * LORA].astype(BF16)
    g_pre = pre_norm_g[0][None, :]

    x2d = x.reshape(batch * seq, D_MODEL)
    p_main, lora_main = _in_proj(x2d, g_pre, w_main, w_lo, tm_in, tn_in)
    meta_rows = jnp.zeros((CHUNK, D_MODEL), F32).at[CHUNK - N_META:].set(meta_tokens.astype(F32))
    p_meta, lora_meta = _in_proj(meta_rows, g_pre, w_main, w_lo, CHUNK, tn_in)

    mu = rwkv_mu[0]
    mu3 = mu[:3 * WIDTH].reshape(3, WIDTH)
    mu_lora = mu[3 * WIDTH:][None, :]
    as_row = lambda t: t[0].reshape(1, WIDTH)
    rows_vec = (mu3, as_row(rwkv_w0), as_row(rwkv_a0), as_row(rwkv_k_k), as_row(rwkv_k_a),
                as_row(rwkv_r_k), as_row(rwkv_gn_g), as_row(rwkv_gn_b))
    w_lora_up = jnp.concatenate([rwkv_w_up[0], rwkv_a_up[0]], axis=0).astype(BF16)
    idx = jnp.arange(256)
    e256 = (idx[:, None] // HEAD_DIM == idx[None, :] // HEAD_DIM).astype(BF16)
    y_rwkv = _rwkv(p_main, lora_main, p_meta, lora_meta, rows_vec, mu_lora, w_lora_up, e256,
                   batch, seq)

    kidx = jnp.arange(tk)
    uo = jnp.concatenate([(kidx[:, None] > kidx[None, :]), jnp.ones((tk, tk), bool)],
                         axis=1).astype(BF16)
    pad = jnp.zeros((tk - N_META, WIDTH), BF16)
    meta_seg = lambda seg: jnp.concatenate(
        [p_meta[CHUNK - N_META:, seg * WIDTH:(seg + 1) * WIDTH], pad], axis=0)
    y_sb = _sb_attn(p_main, meta_seg(SEG_KS), meta_seg(SEG_VS), uo, batch, seq, tq, tk)

    out = _out_proj(y_rwkv, y_sb, p_main, x2d, w_proj_rwkv[0].astype(BF16),
                    w_proj_sb[0].astype(BF16), w_out[0].astype(BF16), post_norm_g[0][None, :],
                    tm_out)
    return out.reshape(batch, seq, D_MODEL)
```

```python
import functools

import jax
import jax.numpy as jnp
from jax import lax
from jax.experimental import pallas as pl
from jax.experimental.pallas import tpu as pltpu

F32 = jnp.float32
BF16 = jnp.bfloat16

D_MODEL = 1024
N_META = 16
HEADS = 16
HEAD_DIM = 64
WIDTH = HEADS * HEAD_DIM
LORA = 64
RMS_EPS = 1e-6
GN_EPS = 64e-5
L2_EPS = 1e-12
DECAY_SCALE = 0.6065306597126334
LOG2E = 1.4426950408889634

LANES = 128
PAIRS = WIDTH // LANES
CHUNK = 64
SUB = 256
N_SEG = 10
SEG_R, SEG_K, SEG_V, SEG_GR, SEG_Q, SEG_KS, SEG_VS, SEG_GS, SEG_MR, SEG_MS = range(N_SEG)
VMEM_LIMIT = 56 * 1024 * 1024


def _dot(a, b):
    return jnp.dot(a.astype(BF16), b.astype(BF16), preferred_element_type=F32)


def _dot_nt(a, b):
    return lax.dot_general(a.astype(BF16), b.astype(BF16), (((1,), (1,)), ((), ())),
                           preferred_element_type=F32)


def _sigmoid(x):
    return 1.0 / (1.0 + jnp.exp(-x))


def _in_proj_kernel(x_ref, g_ref, w_ref, wl_ref, p_ref, lora_ref, hn_ref):
    @pl.when(pl.program_id(1) == 0)
    def _():
        x = x_ref[...]
        inv = lax.rsqrt(jnp.mean(x * x, axis=-1, keepdims=True) + RMS_EPS)
        hn = (x * inv * g_ref[...]).astype(BF16)
        hn_ref[...] = hn
        lora_ref[...] = jnp.dot(hn, wl_ref[...], preferred_element_type=F32)

    p_ref[...] = jnp.dot(hn_ref[...], w_ref[...], preferred_element_type=F32).astype(p_ref.dtype)


def _in_proj(x2d, g, w_main, w_lora, tm, tn):
    rows = x2d.shape[0]
    cols = w_main.shape[1]
    return pl.pallas_call(
        _in_proj_kernel,
        grid=(rows // tm, cols // tn),
        in_specs=[
            pl.BlockSpec((tm, D_MODEL), lambda i, j: (i, 0)),
            pl.BlockSpec((1, D_MODEL), lambda i, j: (0, 0)),
            pl.BlockSpec((D_MODEL, tn), lambda i, j: (0, j)),
            pl.BlockSpec((D_MODEL, 2 * LORA), lambda i, j: (0, 0)),
        ],
        out_specs=[
            pl.BlockSpec((tm, tn), lambda i, j: (i, j)),
            pl.BlockSpec((tm, 2 * LORA), lambda i, j: (i, 0)),
        ],
        out_shape=[
            jax.ShapeDtypeStruct((rows, cols), BF16),
            jax.ShapeDtypeStruct((rows, 2 * LORA), F32),
        ],
        scratch_shapes=[pltpu.VMEM((tm, D_MODEL), BF16)],
        compiler_params=pltpu.CompilerParams(
            dimension_semantics=("arbitrary", "arbitrary"), vmem_limit_bytes=VMEM_LIMIT),
        name="in_proj",
    )(x2d, g, w_main, w_lora)


def _head_sum(x, e256):
    groups = WIDTH // 256
    stacked = jnp.concatenate([x[:, 256 * g:256 * (g + 1)] for g in range(groups)], axis=0)
    s = _dot(stacked, e256)
    return jnp.concatenate([s[CHUNK * g:CHUNK * (g + 1), :] for g in range(groups)], axis=1)


def _rwkv_kernel(r_ref, k_ref, v_ref, gr_ref, lo_ref, pr_ref, pk_ref, pv_ref, plo_ref,
                 mu_ref, mul_ref, w0_ref, a0_ref, kk_ref, ka_ref, rk_ref, gng_ref, gnb_ref,
                 wl_ref, e256_ref, o_ref, s_ref, last_ref, lastl_ref):
    c = pl.program_id(1)
    is_meta = c == 0

    @pl.when(is_meta)
    def _():
        s_ref[...] = jnp.zeros_like(s_ref)
        last_ref[...] = jnp.zeros_like(last_ref)
        lastl_ref[...] = jnp.zeros_like(lastl_ref)

    def pick(meta_ref, main_ref):
        return jnp.where(is_meta, meta_ref[...].astype(F32), main_ref[...].astype(F32))

    row = lax.broadcasted_iota(jnp.int32, (CHUNK, 1), 0)

    def shift(cur, last_row, mu):
        prev = jnp.where(row == 0, last_row, pltpu.roll(cur, 1, 0))
        return cur + (prev - cur) * mu

    r_raw = pick(pr_ref, r_ref)
    k_raw = pick(pk_ref, k_ref)
    v_raw = pick(pv_ref, v_ref)
    lo_raw = pick(plo_ref, lo_ref)
    r = shift(r_raw, last_ref[0:1, :], mu_ref[0:1, :])
    k = shift(k_raw, last_ref[1:2, :], mu_ref[1:2, :])
    v = shift(v_raw, last_ref[2:3, :], mu_ref[2:3, :])
    lo = shift(lo_raw, lastl_ref[0:1, :], mul_ref[...])
    last_ref[0:1, :] = r_raw[CHUNK - 1:CHUNK, :]
    last_ref[1:2, :] = k_raw[CHUNK - 1:CHUNK, :]
    last_ref[2:3, :] = v_raw[CHUNK - 1:CHUNK, :]
    lastl_ref[0:1, :] = lo_raw[CHUNK - 1:CHUNK, :]

    lane_l = lax.broadcasted_iota(jnp.int32, (CHUNK, 2 * LORA), 1)
    pre_w = _dot(jnp.where(lane_l < LORA, jnp.tanh(lo), 0.0), wl_ref[...])
    pre_a = _dot(jnp.where(lane_l >= LORA, lo, 0.0), wl_ref[...])
    logw = -DECAY_SCALE * _sigmoid(w0_ref[...] + pre_w)
    a = _sigmoid(a0_ref[...] + pre_a)

    ti = lax.broadcasted_iota(jnp.int32, (CHUNK, CHUNK), 0)
    si = lax.broadcasted_iota(jnp.int32, (CHUNK, CHUNK), 1)
    tri = (si <= ti).astype(BF16)
    lw_hi = logw.astype(BF16)
    lw_lo = (logw - lw_hi.astype(F32)).astype(BF16)
    cw = (jnp.dot(tri, lw_hi, preferred_element_type=F32)
          + jnp.dot(tri, lw_lo, preferred_element_type=F32))
    w_in = jnp.exp(cw)
    w_ex = jnp.exp(cw - logw)
    w_inv = jnp.exp(-cw)
    w_end = w_in[CHUNK - 1:CHUNK, :]

    e256 = e256_ref[...]
    kk = k * kk_ref[...]
    kappa = kk * lax.rsqrt(_head_sum(kk * kk, e256) + L2_EPS)
    k_rep = k * (1.0 + (a - 1.0) * ka_ref[...])
    kappa_h = kappa * w_ex
    r_h = r * w_in
    k_h = k_rep * w_inv
    b_h = kappa * a * w_inv

    lane = lax.broadcasted_iota(jnp.int32, (CHUNK, LANES), 1)
    first = lane < HEAD_DIM
    col = jnp.where(first, lane, lane - HEAD_DIM)
    strict = col < row
    incl = col <= row
    row2 = lax.broadcasted_iota(jnp.int32, (2 * CHUNK, LANES), 0)
    lane2 = lax.broadcasted_iota(jnp.int32, (2 * CHUNK, LANES), 1)
    same_head = (row2 < HEAD_DIM) == (lane2 < HEAD_DIM)
    eye2 = (row2 == lane2).astype(F32)

    def blockdiag(t):
        return jnp.concatenate([jnp.where(first, t, 0.0), jnp.where(first, 0.0, t)], axis=0)

    pairs = range(PAIRS)
    sls = [slice(LANES * p, LANES * (p + 1)) for p in pairs]
    s_bd = [s_ref[p] for p in pairs]
    lhs = [jnp.concatenate([kappa_h[:, s], r_h[:, s]], axis=0).astype(BF16) for s in sls]
    kb = [jnp.concatenate([k_h[:, s], b_h[:, s]], axis=0).astype(BF16) for s in sls]
    gram = [_dot_nt(lhs[p], jnp.concatenate(
        [blockdiag(k_h[:, sls[p]]), blockdiag(b_h[:, sls[p]]), s_bd[p]], axis=0)) for p in pairs]
    l_bd = [blockdiag(jnp.where(strict, g[:CHUNK, LANES:2 * LANES], 0.0)) for g in gram]
    mk = [jnp.concatenate([jnp.where(strict, g[:CHUNK, :LANES], 0.0),
                           jnp.where(incl, g[CHUNK:, :LANES], 0.0)], axis=0) for g in gram]
    mb_r = [jnp.where(incl, g[CHUNK:, LANES:2 * LANES], 0.0) for g in gram]
    sp = [g[:, 2 * LANES:] for g in gram]
    mv = [_dot(mk[p], blockdiag(v[:, sls[p]])) for p in pairs]

    t_inv = None
    for level in range(CHUNK.bit_length() - 1):
        rb = jnp.right_shift(row2, level)
        off = ((rb & 1) == 1) & (jnp.right_shift(lane2, level) == rb - 1)
        l_off = [jnp.where(off, l, 0.0) for l in l_bd]
        if level == 0:
            t_inv = [eye2 - l for l in l_off]
        else:
            lt = [_dot(l_off[p], t_inv[p]) for p in pairs]
            t_inv = [t_inv[p] - _dot(t_inv[p], lt[p]) for p in pairs]

    rhs = [sp[p][:CHUNK] + mv[p][:CHUNK] for p in pairs]
    u_bd = [_dot(t_inv[p], blockdiag(rhs[p])) for p in pairs]
    vu_t = [jnp.concatenate([v[:, sls[p]], -(u_bd[p][:CHUNK] + u_bd[p][CHUNK:])], axis=0).T
            for p in pairs]
    ds = [_dot(vu_t[p], kb[p]) for p in pairs]
    for p in pairs:
        s_ref[p] = (s_bd[p] + jnp.where(same_head, ds[p], 0.0)) * w_end[:, sls[p]]
    ys = [sp[p][CHUNK:] + mv[p][CHUNK:] - _dot(mb_r[p], u_bd[p]) for p in pairs]

    y = jnp.concatenate(ys, axis=1)
    mean = _head_sum(y, e256) * (1.0 / HEAD_DIM)
    d = y - mean
    var = _head_sum(d * d, e256) * (1.0 / HEAD_DIM)
    y = d * lax.rsqrt(var + GN_EPS) * gng_ref[...] + gnb_ref[...]
    bonus = _head_sum(r * k_rep * rk_ref[...], e256) * v
    g = jnp.where(is_meta, 0.0, gr_ref[...].astype(F32))
    o_ref[...] = ((y + bonus) * (g * _sigmoid(g))).astype(o_ref.dtype)


def _rwkv(p_main, lora_main, p_meta, lora_meta, rows_vec, mu_lora, w_lora, e256, batch, seq):
    n_chunks = seq // CHUNK
    per_b = seq // CHUNK

    def main_map(seg):
        return lambda b, c: (b * per_b + jnp.maximum(c - 1, 0), seg)

    def meta_map(seg):
        return lambda b, c: (0, seg)

    row_spec = pl.BlockSpec((1, WIDTH), lambda b, c: (0, 0))
    in_specs = [
        pl.BlockSpec((CHUNK, WIDTH), main_map(SEG_R)),
        pl.BlockSpec((CHUNK, WIDTH), main_map(SEG_K)),
        pl.BlockSpec((CHUNK, WIDTH), main_map(SEG_V)),
        pl.BlockSpec((CHUNK, WIDTH), main_map(SEG_GR)),
        pl.BlockSpec((CHUNK, 2 * LORA), main_map(0)),
        pl.BlockSpec((CHUNK, WIDTH), meta_map(SEG_R)),
        pl.BlockSpec((CHUNK, WIDTH), meta_map(SEG_K)),
        pl.BlockSpec((CHUNK, WIDTH), meta_map(SEG_V)),
        pl.BlockSpec((CHUNK, 2 * LORA), meta_map(0)),
        pl.BlockSpec((3, WIDTH), lambda b, c: (0, 0)),
        pl.BlockSpec((1, 2 * LORA), lambda b, c: (0, 0)),
    ] + [row_spec] * 7 + [
        pl.BlockSpec((2 * LORA, WIDTH), lambda b, c: (0, 0)),
        pl.BlockSpec((256, 256), lambda b, c: (0, 0)),
    ]
    mu3, w0, a0, k_k, k_a, r_k, gn_g, gn_b = rows_vec
    return pl.pallas_call(
        _rwkv_kernel,
        grid=(batch, n_chunks + 1),
        in_specs=in_specs,
        out_specs=pl.BlockSpec((CHUNK, WIDTH), main_map(0)),
        out_shape=jax.ShapeDtypeStruct((batch * seq, WIDTH), BF16),
        scratch_shapes=[
            pltpu.VMEM((PAIRS, LANES, LANES), F32),
            pltpu.VMEM((8, WIDTH), F32),
            pltpu.VMEM((8, 2 * LORA), F32),
        ],
        compiler_params=pltpu.CompilerParams(
            dimension_semantics=("arbitrary", "arbitrary"), vmem_limit_bytes=VMEM_LIMIT),
        name="rwkv",
    )(p_main, p_main, p_main, p_main, lora_main, p_meta, p_meta, p_meta, lora_meta,
      mu3, mu_lora, w0, a0, k_k, k_a, r_k, gn_g, gn_b, w_lora, e256)


def _sb_kernel(q_ref, k_ref, v_ref, km_ref, vm_ref, g_ref, u_ref, o_ref, *, tq):
    i = pl.program_id(2)
    lane_q = lax.broadcasted_iota(jnp.int32, (tq, LANES), 1)
    q = q_ref[...].astype(F32) * (HEAD_DIM ** -0.5)
    qs = jnp.concatenate([jnp.where(lane_q < HEAD_DIM, q, 0.0),
                          jnp.where(lane_q < HEAD_DIM, 0.0, q)], axis=0).astype(BF16)
    u = u_ref[...]

    def group(kb, vb, visible, carry, acc):
        width = kb.shape[0]
        z = _dot_nt(qs, kb)
        sp = jnp.maximum(z, 0.0) + jnp.log(1.0 + jnp.exp2(jnp.abs(z) * (-LOG2E)))
        if visible is not None:
            sp = jnp.where(visible, sp, 0.0)
        spb = sp.astype(BF16)
        order = range(width // SUB - 1, -1, -1)
        cs_all = jnp.dot(jnp.concatenate([spb[:, SUB * n:SUB * (n + 1)] for n in order], axis=0), u,
                         preferred_element_type=F32)
        lane_v = lax.broadcasted_iota(jnp.int32, (SUB, LANES), 1)
        for idx, n in enumerate(order):
            cs = cs_all[2 * tq * idx:2 * tq * (idx + 1)]
            sl = slice(SUB * n, SUB * (n + 1))
            w = jnp.exp(z[:, sl] - sp[:, sl] - cs)
            if visible is not None:
                w = jnp.where(visible[:, sl], w, 0.0)
            wcat = jnp.concatenate([w[:tq], w[tq:]], axis=1).astype(BF16)
            vbd = jnp.concatenate([jnp.where(lane_v < HEAD_DIM, vb[sl], 0),
                                   jnp.where(lane_v < HEAD_DIM, 0, vb[sl])], axis=0)
            nearer = jnp.exp(-carry)
            acc = acc + (jnp.dot(wcat, vbd, preferred_element_type=F32)
                         * jnp.where(lane_q < HEAD_DIM, nearer[:tq], nearer[tq:]))
            carry = carry + cs[:, 0:1] + sp[:, SUB * n:SUB * n + 1]
        return carry, acc

    def body(j, state):
        start = pl.multiple_of((i - 1 - j) * tq, tq)
        return group(k_ref[pl.ds(start, tq), :], v_ref[pl.ds(start, tq), :], None, *state)

    row = lax.broadcasted_iota(jnp.int32, (2 * tq, tq), 0)
    lane_k = lax.broadcasted_iota(jnp.int32, (2 * tq, tq), 1)
    causal = lane_k < jnp.where(row < tq, row, row - tq)
    start = pl.multiple_of(i * tq, tq)
    state = group(k_ref[pl.ds(start, tq), :], v_ref[pl.ds(start, tq), :], causal,
                  jnp.zeros((2 * tq, 1), F32), jnp.zeros((tq, LANES), F32))
    state = lax.fori_loop(0, i, body, state)
    lane_m = lax.broadcasted_iota(jnp.int32, (2 * tq, SUB), 1)
    _, acc = group(km_ref[...], vm_ref[...], lane_m < N_META, *state)
    g = g_ref[...].astype(F32)
    o_ref[...] = (acc * (g * _sigmoid(g))).astype(o_ref.dtype)


def _sb_attn(p_main, k_meta, v_meta, u, batch, seq, tq):
    nq = seq // tq

    def seq_map(seg):
        return lambda b, p, i: (b, seg * PAIRS + p)

    def tile_map(seg):
        return lambda b, p, i: (b * nq + i, seg * PAIRS + p)

    return pl.pallas_call(
        functools.partial(_sb_kernel, tq=tq),
        grid=(batch, PAIRS, nq),
        in_specs=[
            pl.BlockSpec((tq, LANES), tile_map(SEG_Q)),
            pl.BlockSpec((seq, LANES), seq_map(SEG_KS)),
            pl.BlockSpec((seq, LANES), seq_map(SEG_VS)),
            pl.BlockSpec((SUB, LANES), lambda b, p, i: (0, p)),
            pl.BlockSpec((SUB, LANES), lambda b, p, i: (0, p)),
            pl.BlockSpec((tq, LANES), tile_map(SEG_GS)),
            pl.BlockSpec((SUB, SUB), lambda b, p, i: (0, 0)),
        ],
        out_specs=pl.BlockSpec((tq, LANES), lambda b, p, i: (b * nq + i, p)),
        out_shape=jax.ShapeDtypeStruct((batch * seq, WIDTH), BF16),
        compiler_params=pltpu.CompilerParams(
            dimension_semantics=("arbitrary", "arbitrary", "arbitrary"),
            vmem_limit_bytes=VMEM_LIMIT),
        name="sb_attn",
    )(p_main, p_main, p_main, k_meta, v_meta, p_main, u)


def _out_kernel(yr_ref, ys_ref, mr_ref, ms_ref, x_ref, wr_ref, ws_ref, wo_ref, g_ref, o_ref):
    pr = jnp.dot(yr_ref[...], wr_ref[...], preferred_element_type=F32)
    ps = jnp.dot(ys_ref[...], ws_ref[...], preferred_element_type=F32)
    mixed = (_sigmoid(mr_ref[...].astype(F32)) * pr + _sigmoid(ms_ref[...].astype(F32)) * ps)
    o = jnp.dot(mixed.astype(BF16), wo_ref[...], preferred_element_type=F32)
    inv = lax.rsqrt(jnp.mean(o * o, axis=-1, keepdims=True) + RMS_EPS)
    o_ref[...] = x_ref[...] + o * inv * g_ref[...]


def _out_proj(y_rwkv, y_sb, p_main, x2d, w_r, w_s, w_o, g, tm):
    rows = x2d.shape[0]
    w_spec = pl.BlockSpec((WIDTH, D_MODEL), lambda i: (0, 0))
    return pl.pallas_call(
        _out_kernel,
        grid=(rows // tm,),
        in_specs=[
            pl.BlockSpec((tm, WIDTH), lambda i: (i, 0)),
            pl.BlockSpec((tm, WIDTH), lambda i: (i, 0)),
            pl.BlockSpec((tm, WIDTH), lambda i: (i, SEG_MR)),
            pl.BlockSpec((tm, WIDTH), lambda i: (i, SEG_MS)),
            pl.BlockSpec((tm, D_MODEL), lambda i: (i, 0)),
            w_spec, w_spec, w_spec,
            pl.BlockSpec((1, D_MODEL), lambda i: (0, 0)),
        ],
        out_specs=pl.BlockSpec((tm, D_MODEL), lambda i: (i, 0)),
        out_shape=jax.ShapeDtypeStruct((rows, D_MODEL), F32),
        compiler_params=pltpu.CompilerParams(
            dimension_semantics=("arbitrary",), vmem_limit_bytes=VMEM_LIMIT),
        name="out_proj",
    )(y_rwkv, y_sb, p_main, p_main, x2d, w_r, w_s, w_o, g)


def _tiles(batch, seq):
    rows = batch * seq
    tm_in = min(1024, rows)
    tm_out = min(512, rows)
    tq = min(512, seq)
    return tm_in, 2048, tm_out, tq


def kernel(x, meta_tokens, pre_norm_g, post_norm_g, w_in, rwkv_mu, rwkv_w0, rwkv_w_up, rwkv_a0,
           rwkv_a_up, rwkv_k_k, rwkv_k_a, rwkv_r_k, rwkv_gn_g, rwkv_gn_b, w_proj_rwkv, w_proj_sb,
           w_out):
    batch, seq, d_model = x.shape
    assert d_model == D_MODEL and w_in.shape[0] == 1, "single-layer kernel"
    assert seq % 512 == 0 and meta_tokens.shape == (N_META, D_MODEL)
    tm_in, tn_in, tm_out, tq = _tiles(batch, seq)

    w = w_in[0]
    w_main = jnp.concatenate([w[:, :3 * WIDTH], w[:, 3 * WIDTH + 2 * LORA:]], axis=1).astype(BF16)
    w_lo = w[:, 3 * WIDTH:3 * WIDTH + 2 * LORA].astype(BF16)
    g_pre = pre_norm_g[0][None, :]

    x2d = x.reshape(batch * seq, D_MODEL)
    p_main, lora_main = _in_proj(x2d, g_pre, w_main, w_lo, tm_in, tn_in)
    meta_rows = jnp.zeros((CHUNK, D_MODEL), F32).at[CHUNK - N_META:].set(meta_tokens.astype(F32))
    p_meta, lora_meta = _in_proj(meta_rows, g_pre, w_main, w_lo, CHUNK, tn_in)

    mu = rwkv_mu[0]
    mu3 = mu[:3 * WIDTH].reshape(3, WIDTH)
    mu_lora = mu[3 * WIDTH:][None, :]
    as_row = lambda t: t[0].reshape(1, WIDTH)
    rows_vec = (mu3, as_row(rwkv_w0), as_row(rwkv_a0), as_row(rwkv_k_k), as_row(rwkv_k_a),
                as_row(rwkv_r_k), as_row(rwkv_gn_g), as_row(rwkv_gn_b))
    w_lora_up = jnp.concatenate([rwkv_w_up[0], rwkv_a_up[0]], axis=0).astype(BF16)
    idx = jnp.arange(256)
    e256 = (idx[:, None] // HEAD_DIM == idx[None, :] // HEAD_DIM).astype(BF16)
    y_rwkv = _rwkv(p_main, lora_main, p_meta, lora_meta, rows_vec, mu_lora, w_lora_up, e256,
                   batch, seq)

    kidx = jnp.arange(SUB)
    u_tri = (kidx[:, None] > kidx[None, :]).astype(BF16)
    pad = jnp.zeros((SUB - N_META, WIDTH), BF16)
    meta_seg = lambda seg: jnp.concatenate(
        [p_meta[CHUNK - N_META:, seg * WIDTH:(seg + 1) * WIDTH], pad], axis=0)
    y_sb = _sb_attn(p_main, meta_seg(SEG_KS), meta_seg(SEG_VS), u_tri, batch, seq, tq)

    out = _out_proj(y_rwkv, y_sb, p_main, x2d, w_proj_rwkv[0].astype(BF16),
                    w_proj_sb[0].astype(BF16), w_out[0].astype(BF16), post_norm_g[0][None, :],
                    tm_out)
    return out.reshape(batch, seq, D_MODEL)
```

```python
import functools

import jax
import jax.numpy as jnp
from jax import lax
from jax.experimental import pallas as pl
from jax.experimental.pallas import tpu as pltpu

F32 = jnp.float32
BF16 = jnp.bfloat16

D_MODEL = 1024
N_META = 16
HEADS = 16
HEAD_DIM = 64
WIDTH = HEADS * HEAD_DIM
LORA = 64
RMS_EPS = 1e-6
GN_EPS = 64e-5
L2_EPS = 1e-12
DECAY_SCALE = 0.6065306597126334
LOG2E = 1.4426950408889634

LANES = 128
PAIRS = WIDTH // LANES
CHUNK = 64
RW_CHUNKS = 2
RW_TOK = RW_CHUNKS * CHUNK
SUB = 256
DEAD_CARRY = 110.0
N_SEG = 10
SEG_R, SEG_K, SEG_V, SEG_GR, SEG_Q, SEG_KS, SEG_VS, SEG_GS, SEG_MR, SEG_MS = range(N_SEG)
VMEM_LIMIT = 56 * 1024 * 1024


def _dot(a, b):
    return jnp.dot(a.astype(BF16), b.astype(BF16), preferred_element_type=F32)


def _dot_nt(a, b):
    return lax.dot_general(a.astype(BF16), b.astype(BF16), (((1,), (1,)), ((), ())),
                           preferred_element_type=F32)


def _sigmoid(x):
    return 1.0 / (1.0 + jnp.exp(-x))


def _in_proj_kernel(x_ref, g_ref, w_ref, wl_ref, p_ref, lora_ref, hn_ref):
    @pl.when(pl.program_id(1) == 0)
    def _():
        x = x_ref[...]
        inv = lax.rsqrt(jnp.mean(x * x, axis=-1, keepdims=True) + RMS_EPS)
        hn = (x * inv * g_ref[...]).astype(BF16)
        hn_ref[...] = hn
        lora_ref[...] = jnp.dot(hn, wl_ref[...], preferred_element_type=F32)

    p_ref[...] = jnp.dot(hn_ref[...], w_ref[...], preferred_element_type=F32).astype(p_ref.dtype)


def _in_proj(x2d, g, w_main, w_lora, tm, tn):
    rows = x2d.shape[0]
    cols = w_main.shape[1]
    return pl.pallas_call(
        _in_proj_kernel,
        grid=(rows // tm, cols // tn),
        in_specs=[
            pl.BlockSpec((tm, D_MODEL), lambda i, j: (i, 0)),
            pl.BlockSpec((1, D_MODEL), lambda i, j: (0, 0)),
            pl.BlockSpec((D_MODEL, tn), lambda i, j: (0, j)),
            pl.BlockSpec((D_MODEL, 2 * LORA), lambda i, j: (0, 0)),
        ],
        out_specs=[
            pl.BlockSpec((tm, tn), lambda i, j: (i, j)),
            pl.BlockSpec((tm, 2 * LORA), lambda i, j: (i, 0)),
        ],
        out_shape=[
            jax.ShapeDtypeStruct((rows, cols), BF16),
            jax.ShapeDtypeStruct((rows, 2 * LORA), F32),
        ],
        scratch_shapes=[pltpu.VMEM((tm, D_MODEL), BF16)],
        compiler_params=pltpu.CompilerParams(
            dimension_semantics=("arbitrary", "arbitrary"), vmem_limit_bytes=VMEM_LIMIT),
        name="in_proj",
    )(x2d, g, w_main, w_lora)


def _head_sum(x, e256):
    rows = x.shape[0]
    groups = WIDTH // 256
    stacked = jnp.concatenate([x[:, 256 * g:256 * (g + 1)] for g in range(groups)], axis=0)
    s = _dot(stacked, e256)
    return jnp.concatenate([s[rows * g:rows * (g + 1), :] for g in range(groups)], axis=1)


def _rwkv_kernel(r_ref, k_ref, v_ref, gr_ref, lo_ref, pr_ref, pk_ref, pv_ref, plo_ref,
                 mu_ref, mul_ref, w0_ref, a0_ref, kk_ref, ka_ref, rk_ref, gng_ref, gnb_ref,
                 wl_ref, e256_ref, o_ref, s_ref, last_ref, lastl_ref):
    c = pl.program_id(1)
    is_meta = c == 0

    @pl.when(is_meta)
    def _():
        s_ref[...] = jnp.zeros_like(s_ref)
        last_ref[...] = jnp.zeros_like(last_ref)
        lastl_ref[...] = jnp.zeros_like(lastl_ref)

    def pick(meta_ref, main_ref):
        return jnp.where(is_meta, meta_ref[...].astype(F32), main_ref[...].astype(F32))

    tok = lax.broadcasted_iota(jnp.int32, (RW_TOK, 1), 0)

    def shift(cur, last_row, mu):
        prev = jnp.where(tok == 0, last_row, pltpu.roll(cur, 1, 0))
        return cur + (prev - cur) * mu

    r_raw = pick(pr_ref, r_ref)
    k_raw = pick(pk_ref, k_ref)
    v_raw = pick(pv_ref, v_ref)
    lo_raw = pick(plo_ref, lo_ref)
    r = shift(r_raw, last_ref[0:1, :], mu_ref[0:1, :])
    k = shift(k_raw, last_ref[1:2, :], mu_ref[1:2, :])
    v = shift(v_raw, last_ref[2:3, :], mu_ref[2:3, :])
    lo = shift(lo_raw, lastl_ref[0:1, :], mul_ref[...])
    last_ref[0:1, :] = r_raw[RW_TOK - 1:RW_TOK, :]
    last_ref[1:2, :] = k_raw[RW_TOK - 1:RW_TOK, :]
    last_ref[2:3, :] = v_raw[RW_TOK - 1:RW_TOK, :]
    lastl_ref[0:1, :] = lo_raw[RW_TOK - 1:RW_TOK, :]

    lane_l = lax.broadcasted_iota(jnp.int32, (RW_TOK, 2 * LORA), 1)
    pre_w = _dot(jnp.where(lane_l < LORA, jnp.tanh(lo), 0.0), wl_ref[...])
    pre_a = _dot(jnp.where(lane_l >= LORA, lo, 0.0), wl_ref[...])
    logw = -DECAY_SCALE * _sigmoid(w0_ref[...] + pre_w)
    a = _sigmoid(a0_ref[...] + pre_a)

    ti = lax.broadcasted_iota(jnp.int32, (RW_TOK, RW_TOK), 0)
    si = lax.broadcasted_iota(jnp.int32, (RW_TOK, RW_TOK), 1)
    chunk_shift = CHUNK.bit_length() - 1
    tri = ((si <= ti) & (jnp.right_shift(si, chunk_shift) == jnp.right_shift(ti, chunk_shift))
           ).astype(BF16)
    lw_hi = logw.astype(BF16)
    lw_lo = (logw - lw_hi.astype(F32)).astype(BF16)
    cw = (jnp.dot(tri, lw_hi, preferred_element_type=F32)
          + jnp.dot(tri, lw_lo, preferred_element_type=F32))
    w_in = jnp.exp(cw)
    w_ex = jnp.exp(cw - logw)
    w_inv = jnp.exp(-cw)

    e256 = e256_ref[...]
    kk = k * kk_ref[...]
    kappa = kk * lax.rsqrt(_head_sum(kk * kk, e256) + L2_EPS)
    k_rep = k * (1.0 + (a - 1.0) * ka_ref[...])
    kappa_h = kappa * w_ex
    r_h = r * w_in
    k_h = k_rep * w_inv
    b_h = kappa * a * w_inv

    row = lax.broadcasted_iota(jnp.int32, (CHUNK, 1), 0)
    lane = lax.broadcasted_iota(jnp.int32, (CHUNK, LANES), 1)
    first = lane < HEAD_DIM
    col = jnp.where(first, lane, lane - HEAD_DIM)
    strict = col < row
    incl = col <= row
    row2 = lax.broadcasted_iota(jnp.int32, (2 * CHUNK, LANES), 0)
    lane2 = lax.broadcasted_iota(jnp.int32, (2 * CHUNK, LANES), 1)
    same_head = (row2 < HEAD_DIM) == (lane2 < HEAD_DIM)
    eye2 = (row2 == lane2).astype(F32)

    def blockdiag(t):
        return jnp.concatenate([jnp.where(first, t, 0.0), jnp.where(first, 0.0, t)], axis=0)

    def part(x, ch, p):
        return x[CHUNK * ch:CHUNK * (ch + 1), LANES * p:LANES * (p + 1)]

    pairs = range(PAIRS)
    cps = [(ch, p) for ch in range(RW_CHUNKS) for p in pairs]
    lhs = {cp: jnp.concatenate([part(kappa_h, *cp), part(r_h, *cp)], axis=0).astype(BF16)
           for cp in cps}
    kb = {cp: jnp.concatenate([part(k_h, *cp), part(b_h, *cp)], axis=0).astype(BF16) for cp in cps}
    gram = {cp: _dot_nt(lhs[cp], jnp.concatenate(
        [blockdiag(part(k_h, *cp)), blockdiag(part(b_h, *cp))], axis=0)) for cp in cps}
    l_bd = {cp: blockdiag(jnp.where(strict, gram[cp][:CHUNK, LANES:], 0.0)) for cp in cps}
    mk = {cp: jnp.concatenate([jnp.where(strict, gram[cp][:CHUNK, :LANES], 0.0),
                               jnp.where(incl, gram[cp][CHUNK:, :LANES], 0.0)], axis=0)
          for cp in cps}
    mb_r = {cp: jnp.where(incl, gram[cp][CHUNK:, LANES:], 0.0) for cp in cps}
    mv = {cp: _dot(mk[cp], blockdiag(part(v, *cp))) for cp in cps}

    t_inv = None
    for level in range(chunk_shift):
        rb = jnp.right_shift(row2, level)
        off = ((rb & 1) == 1) & (jnp.right_shift(lane2, level) == rb - 1)
        l_off = {cp: jnp.where(off, l_bd[cp], 0.0) for cp in cps}
        if level == 0:
            t_inv = {cp: eye2 - l_off[cp] for cp in cps}
        else:
            lt = {cp: _dot(l_off[cp], t_inv[cp]) for cp in cps}
            t_inv = {cp: t_inv[cp] - _dot(t_inv[cp], lt[cp]) for cp in cps}

    state = [s_ref[p] for p in pairs]
    y_chunks = []
    for ch in range(RW_CHUNKS):
        w_end = w_in[CHUNK * (ch + 1) - 1:CHUNK * (ch + 1), :]
        sp = [_dot_nt(lhs[ch, p], state[p]) for p in pairs]
        u_bd = [_dot(t_inv[ch, p], blockdiag(sp[p][:CHUNK] + mv[ch, p][:CHUNK])) for p in pairs]
        vu_t = [jnp.concatenate([part(v, ch, p), -(u_bd[p][:CHUNK] + u_bd[p][CHUNK:])], axis=0).T
                for p in pairs]
        ds = [_dot(vu_t[p], kb[ch, p]) for p in pairs]
        state = [(state[p] + jnp.where(same_head, ds[p], 0.0)) * w_end[:, LANES * p:LANES * (p + 1)]
                 for p in pairs]
        y_chunks.append(jnp.concatenate(
            [sp[p][CHUNK:] + mv[ch, p][CHUNK:] - _dot(mb_r[ch, p], u_bd[p]) for p in pairs], axis=1))
    for p in pairs:
        s_ref[p] = state[p]

    y = jnp.concatenate(y_chunks, axis=0)
    mean = _head_sum(y, e256) * (1.0 / HEAD_DIM)
    d = y - mean
    var = _head_sum(d * d, e256) * (1.0 / HEAD_DIM)
    y = d * lax.rsqrt(var + GN_EPS) * gng_ref[...] + gnb_ref[...]
    bonus = _head_sum(r * k_rep * rk_ref[...], e256) * v
    g = jnp.where(is_meta, 0.0, gr_ref[...].astype(F32))
    o_ref[...] = ((y + bonus) * (g * _sigmoid(g))).astype(o_ref.dtype)


def _rwkv(p_main, lora_main, p_meta, lora_meta, rows_vec, mu_lora, w_lora, e256, batch, seq):
    per_b = seq // RW_TOK

    def main_map(seg):
        return lambda b, c: (b * per_b + jnp.maximum(c - 1, 0), seg)

    def meta_map(seg):
        return lambda b, c: (0, seg)

    row_spec = pl.BlockSpec((1, WIDTH), lambda b, c: (0, 0))
    in_specs = [
        pl.BlockSpec((RW_TOK, WIDTH), main_map(SEG_R)),
        pl.BlockSpec((RW_TOK, WIDTH), main_map(SEG_K)),
        pl.BlockSpec((RW_TOK, WIDTH), main_map(SEG_V)),
        pl.BlockSpec((RW_TOK, WIDTH), main_map(SEG_GR)),
        pl.BlockSpec((RW_TOK, 2 * LORA), main_map(0)),
        pl.BlockSpec((RW_TOK, WIDTH), meta_map(SEG_R)),
        pl.BlockSpec((RW_TOK, WIDTH), meta_map(SEG_K)),
        pl.BlockSpec((RW_TOK, WIDTH), meta_map(SEG_V)),
        pl.BlockSpec((RW_TOK, 2 * LORA), meta_map(0)),
        pl.BlockSpec((3, WIDTH), lambda b, c: (0, 0)),
        pl.BlockSpec((1, 2 * LORA), lambda b, c: (0, 0)),
    ] + [row_spec] * 7 + [
        pl.BlockSpec((2 * LORA, WIDTH), lambda b, c: (0, 0)),
        pl.BlockSpec((256, 256), lambda b, c: (0, 0)),
    ]
    mu3, w0, a0, k_k, k_a, r_k, gn_g, gn_b = rows_vec
    return pl.pallas_call(
        _rwkv_kernel,
        grid=(batch, per_b + 1),
        in_specs=in_specs,
        out_specs=pl.BlockSpec((RW_TOK, WIDTH), main_map(0)),
        out_shape=jax.ShapeDtypeStruct((batch * seq, WIDTH), BF16),
        scratch_shapes=[
            pltpu.VMEM((PAIRS, LANES, LANES), F32),
            pltpu.VMEM((8, WIDTH), F32),
            pltpu.VMEM((8, 2 * LORA), F32),
        ],
        compiler_params=pltpu.CompilerParams(
            dimension_semantics=("arbitrary", "arbitrary"), vmem_limit_bytes=VMEM_LIMIT),
        name="rwkv",
    )(p_main, p_main, p_main, p_main, lora_main, p_meta, p_meta, p_meta, lora_meta,
      mu3, mu_lora, w0, a0, k_k, k_a, r_k, gn_g, gn_b, w_lora, e256)


def _sb_kernel(q_ref, k_ref, v_ref, km_ref, vm_ref, g_ref, u_ref, o_ref, *, tq):
    i = pl.program_id(2)
    n_sub = tq // SUB
    u = u_ref[...]
    first = lax.broadcasted_iota(jnp.int32, (SUB, LANES), 1) < HEAD_DIM
    row = lax.broadcasted_iota(jnp.int32, (2 * SUB, SUB), 0)
    lane_k = lax.broadcasted_iota(jnp.int32, (2 * SUB, SUB), 1)
    causal = lane_k < jnp.where(row < SUB, row, row - SUB)
    meta_visible = lax.broadcasted_iota(jnp.int32, (2 * SUB, LANES), 1) < N_META

    def stacked_q(s):
        q = q_ref[SUB * s:SUB * (s + 1), :].astype(F32) * (HEAD_DIM ** -0.5)
        return jnp.concatenate([jnp.where(first, q, 0.0), jnp.where(first, 0.0, q)],
                               axis=0).astype(BF16)

    def cores(items):
        zs = [_dot_nt(qs, kb) for qs, kb, _, _ in items]
        sps = []
        for z, (_, _, _, visible) in zip(zs, items):
            sp = jnp.maximum(z, 0.0) + jnp.log(1.0 + jnp.exp2(jnp.abs(z) * (-LOG2E)))
            sps.append(sp if visible is None else jnp.where(visible, sp, 0.0))
        css = [jnp.dot(sp.astype(BF16), u[:sp.shape[1], :sp.shape[1]], preferred_element_type=F32)
               for sp in sps]
        out = []
        for z, sp, cs, (_, _, vb, visible) in zip(zs, sps, css, items):
            w = jnp.exp(z - sp - cs)
            if visible is not None:
                w = jnp.where(visible, w, 0.0)
            wcat = jnp.concatenate([w[:SUB], w[SUB:]], axis=1).astype(BF16)
            first_v = lax.broadcasted_iota(jnp.int32, vb.shape, 1) < HEAD_DIM
            vbd = jnp.concatenate([jnp.where(first_v, vb, 0), jnp.where(first_v, 0, vb)], axis=0)
            out.append((jnp.dot(wcat, vbd, preferred_element_type=F32), cs[:, 0:1] + sp[:, 0:1]))
        return out

    def nearer(carry):
        f = jnp.exp(-carry)
        return jnp.where(first, f[:SUB], f[SUB:])

    def keys(j):
        start = pl.multiple_of(j * SUB, SUB)
        return k_ref[pl.ds(start, SUB), :], v_ref[pl.ds(start, SUB), :]

    qss = [stacked_q(s) for s in range(n_sub)]
    items = []
    for s in range(n_sub):
        ii = i * n_sub + s
        items.append((qss[s],) + keys(ii) + (causal,))
        items.append((qss[s],) + keys(jnp.maximum(ii - 1, 0)) + (None,))
    first_two = cores(items)

    for s in range(n_sub):
        ii = i * n_sub + s
        (pv_d, tot_d), (pv_p, tot_p) = first_two[2 * s], first_two[2 * s + 1]
        has_prev = (ii > 0).astype(F32)
        acc = pv_d + pv_p * (nearer(tot_d) * has_prev)
        carry = tot_d + tot_p * has_prev

        def body(state, s=s):
            j, _, carry, acc = state
            (pv, tot), = cores([(qss[s],) + keys(j) + (None,)])
            carry_new = carry + tot
            return j - 1, jnp.min(carry_new) <= DEAD_CARRY, carry_new, acc + pv * nearer(carry)

        _, alive, carry, acc = lax.while_loop(
            lambda state: (state[0] >= 0) & state[1], body,
            (ii - 2, jnp.min(carry) <= DEAD_CARRY, carry, acc))

        def with_meta(carry, acc, s=s):
            (pv, _), = cores([(qss[s], km_ref[...], vm_ref[...], meta_visible)])
            return acc + pv * nearer(carry)

        acc = lax.cond(alive, with_meta, lambda carry, acc: acc, carry, acc)
        g = g_ref[SUB * s:SUB * (s + 1), :].astype(F32)
        o_ref[SUB * s:SUB * (s + 1), :] = (acc * (g * _sigmoid(g))).astype(o_ref.dtype)


def _sb_attn(p_main, k_meta, v_meta, u, batch, seq, tq):
    nq = seq // tq

    def seq_map(seg):
        return lambda b, p, i: (b, seg * PAIRS + p)

    def tile_map(seg):
        return lambda b, p, i: (b * nq + i, seg * PAIRS + p)

    return pl.pallas_call(
        functools.partial(_sb_kernel, tq=tq),
        grid=(batch, PAIRS, nq),
        in_specs=[
            pl.BlockSpec((tq, LANES), tile_map(SEG_Q)),
            pl.BlockSpec((seq, LANES), seq_map(SEG_KS)),
            pl.BlockSpec((seq, LANES), seq_map(SEG_VS)),
            pl.BlockSpec((LANES, LANES), lambda b, p, i: (0, p)),
            pl.BlockSpec((LANES, LANES), lambda b, p, i: (0, p)),
            pl.BlockSpec((tq, LANES), tile_map(SEG_GS)),
            pl.BlockSpec((SUB, SUB), lambda b, p, i: (0, 0)),
        ],
        out_specs=pl.BlockSpec((tq, LANES), lambda b, p, i: (b * nq + i, p)),
        out_shape=jax.ShapeDtypeStruct((batch * seq, WIDTH), BF16),
        compiler_params=pltpu.CompilerParams(
            dimension_semantics=("arbitrary", "arbitrary", "arbitrary"),
            vmem_limit_bytes=VMEM_LIMIT),
        name="sb_attn",
    )(p_main, p_main, p_main, k_meta, v_meta, p_main, u)


def _out_kernel(yr_ref, ys_ref, mr_ref, ms_ref, x_ref, wr_ref, ws_ref, wo_ref, g_ref, o_ref):
    pr = jnp.dot(yr_ref[...], wr_ref[...], preferred_element_type=F32)
    ps = jnp.dot(ys_ref[...], ws_ref[...], preferred_element_type=F32)
    mixed = (_sigmoid(mr_ref[...].astype(F32)) * pr + _sigmoid(ms_ref[...].astype(F32)) * ps)
    o = jnp.dot(mixed.astype(BF16), wo_ref[...], preferred_element_type=F32)
    inv = lax.rsqrt(jnp.mean(o * o, axis=-1, keepdims=True) + RMS_EPS)
    o_ref[...] = x_ref[...] + o * inv * g_ref[...]


def _out_proj(y_rwkv, y_sb, p_main, x2d, w_r, w_s, w_o, g, tm):
    rows = x2d.shape[0]
    w_spec = pl.BlockSpec((WIDTH, D_MODEL), lambda i: (0, 0))
    return pl.pallas_call(
        _out_kernel,
        grid=(rows // tm,),
        in_specs=[
            pl.BlockSpec((tm, WIDTH), lambda i: (i, 0)),
            pl.BlockSpec((tm, WIDTH), lambda i: (i, 0)),
            pl.BlockSpec((tm, WIDTH), lambda i: (i, SEG_MR)),
            pl.BlockSpec((tm, WIDTH), lambda i: (i, SEG_MS)),
            pl.BlockSpec((tm, D_MODEL), lambda i: (i, 0)),
            w_spec, w_spec, w_spec,
            pl.BlockSpec((1, D_MODEL), lambda i: (0, 0)),
        ],
        out_specs=pl.BlockSpec((tm, D_MODEL), lambda i: (i, 0)),
        out_shape=jax.ShapeDtypeStruct((rows, D_MODEL), F32),
        compiler_params=pltpu.CompilerParams(
            dimension_semantics=("arbitrary",), vmem_limit_bytes=VMEM_LIMIT),
        name="out_proj",
    )(y_rwkv, y_sb, p_main, p_main, x2d, w_r, w_s, w_o, g)


def _tiles(batch, seq):
    rows = batch * seq
    tm_in = min(1024, rows)
    tm_out = min(512, rows)
    tq = min(512, seq)
    return tm_in, 2048, tm_out, tq


def kernel(x, meta_tokens, pre_norm_g, post_norm_g, w_in, rwkv_mu, rwkv_w0, rwkv_w_up, rwkv_a0,
           rwkv_a_up, rwkv_k_k, rwkv_k_a, rwkv_r_k, rwkv_gn_g, rwkv_gn_b, w_proj_rwkv, w_proj_sb,
           w_out):
    batch, seq, d_model = x.shape
    assert d_model == D_MODEL and w_in.shape[0] == 1, "single-layer kernel"
    assert seq % 512 == 0 and meta_tokens.shape == (N_META, D_MODEL)
    tm_in, tn_in, tm_out, tq = _tiles(batch, seq)

    w = w_in[0]
    w_main = jnp.concatenate([w[:, :3 * WIDTH], w[:, 3 * WIDTH + 2 * LORA:]], axis=1).astype(BF16)
    w_lo = w[:, 3 * WIDTH:3 * WIDTH + 2 * LORA].astype(BF16)
    g_pre = pre_norm_g[0][None, :]

    x2d = x.reshape(batch * seq, D_MODEL)
    p_main, lora_main = _in_proj(x2d, g_pre, w_main, w_lo, tm_in, tn_in)
    meta_rows = jnp.zeros((RW_TOK, D_MODEL), F32).at[RW_TOK - N_META:].set(meta_tokens.astype(F32))
    p_meta, lora_meta = _in_proj(meta_rows, g_pre, w_main, w_lo, RW_TOK, tn_in)

    mu = rwkv_mu[0]
    mu3 = mu[:3 * WIDTH].reshape(3, WIDTH)
    mu_lora = mu[3 * WIDTH:][None, :]
    as_row = lambda t: t[0].reshape(1, WIDTH)
    rows_vec = (mu3, as_row(rwkv_w0), as_row(rwkv_a0), as_row(rwkv_k_k), as_row(rwkv_k_a),
                as_row(rwkv_r_k), as_row(rwkv_gn_g), as_row(rwkv_gn_b))
    w_lora_up = jnp.concatenate([rwkv_w_up[0], rwkv_a_up[0]], axis=0).astype(BF16)
    idx = jnp.arange(256)
    e256 = (idx[:, None] // HEAD_DIM == idx[None, :] // HEAD_DIM).astype(BF16)
    y_rwkv = _rwkv(p_main, lora_main, p_meta, lora_meta, rows_vec, mu_lora, w_lora_up, e256,
                   batch, seq)

    kidx = jnp.arange(SUB)
    u_tri = (kidx[:, None] > kidx[None, :]).astype(BF16)
    pad = jnp.zeros((LANES - N_META, WIDTH), BF16)
    meta_seg = lambda seg: jnp.concatenate(
        [p_meta[RW_TOK - N_META:, seg * WIDTH:(seg + 1) * WIDTH], pad], axis=0)
    y_sb = _sb_attn(p_main, meta_seg(SEG_KS), meta_seg(SEG_VS), u_tri, batch, seq, tq)

    out = _out_proj(y_rwkv, y_sb, p_main, x2d, w_proj_rwkv[0].astype(BF16),
                    w_proj_sb[0].astype(BF16), w_out[0].astype(BF16), post_norm_g[0][None, :],
                    tm_out)
    return out.reshape(batch, seq, D_MODEL)
```

```python
import functools

import jax
import jax.numpy as jnp
from jax import lax
from jax.experimental import pallas as pl
from jax.experimental.pallas import tpu as pltpu

F32 = jnp.float32
BF16 = jnp.bfloat16

D_MODEL = 1024
N_META = 16
HEADS = 16
HEAD_DIM = 64
WIDTH = HEADS * HEAD_DIM
LORA = 64
RMS_EPS = 1e-6
GN_EPS = 64e-5
L2_EPS = 1e-12
DECAY_SCALE = 0.6065306597126334
LOG2E = 1.4426950408889634

LANES = 128
PAIRS = WIDTH // LANES
CHUNK = 64
RW_CHUNKS = 4
RW_TOK = RW_CHUNKS * CHUNK
SUB = 256
DEAD_CARRY = 110.0
N_SEG = 10
SEG_R, SEG_K, SEG_V, SEG_GR, SEG_Q, SEG_KS, SEG_VS, SEG_GS, SEG_MR, SEG_MS = range(N_SEG)
VMEM_LIMIT = 56 * 1024 * 1024


def _dot(a, b):
    return jnp.dot(a.astype(BF16), b.astype(BF16), preferred_element_type=F32)


def _dot_nt(a, b):
    return lax.dot_general(a.astype(BF16), b.astype(BF16), (((1,), (1,)), ((), ())),
                           preferred_element_type=F32)


def _sigmoid(x):
    return 1.0 / (1.0 + jnp.exp(-x))


def _in_proj_kernel(x_ref, g_ref, w_ref, wl_ref, p_ref, lora_ref, hn_ref):
    @pl.when(pl.program_id(1) == 0)
    def _():
        x = x_ref[...]
        inv = lax.rsqrt(jnp.mean(x * x, axis=-1, keepdims=True) + RMS_EPS)
        hn = (x * inv * g_ref[...]).astype(BF16)
        hn_ref[...] = hn
        lora_ref[...] = jnp.dot(hn, wl_ref[...], preferred_element_type=F32)

    p_ref[...] = jnp.dot(hn_ref[...], w_ref[...], preferred_element_type=F32).astype(p_ref.dtype)


def _in_proj(x2d, g, w_main, w_lora, tm, tn):
    rows = x2d.shape[0]
    cols = w_main.shape[1]
    return pl.pallas_call(
        _in_proj_kernel,
        grid=(rows // tm, cols // tn),
        in_specs=[
            pl.BlockSpec((tm, D_MODEL), lambda i, j: (i, 0)),
            pl.BlockSpec((1, D_MODEL), lambda i, j: (0, 0)),
            pl.BlockSpec((D_MODEL, tn), lambda i, j: (0, j)),
            pl.BlockSpec((D_MODEL, 2 * LORA), lambda i, j: (0, 0)),
        ],
        out_specs=[
            pl.BlockSpec((tm, tn), lambda i, j: (i, j)),
            pl.BlockSpec((tm, 2 * LORA), lambda i, j: (i, 0)),
        ],
        out_shape=[
            jax.ShapeDtypeStruct((rows, cols), BF16),
            jax.ShapeDtypeStruct((rows, 2 * LORA), F32),
        ],
        scratch_shapes=[pltpu.VMEM((tm, D_MODEL), BF16)],
        compiler_params=pltpu.CompilerParams(
            dimension_semantics=("arbitrary", "arbitrary"), vmem_limit_bytes=VMEM_LIMIT),
        name="in_proj",
    )(x2d, g, w_main, w_lora)


def _head_sum(x, e256):
    rows = x.shape[0]
    groups = WIDTH // 256
    stacked = jnp.concatenate([x[:, 256 * g:256 * (g + 1)] for g in range(groups)], axis=0)
    s = _dot(stacked, e256)
    return jnp.concatenate([s[rows * g:rows * (g + 1), :] for g in range(groups)], axis=1)


def _rwkv_kernel(r_ref, k_ref, v_ref, gr_ref, lo_ref, pr_ref, pk_ref, pv_ref, plo_ref,
                 mu_ref, mul_ref, w0_ref, a0_ref, kk_ref, ka_ref, rk_ref, gng_ref, gnb_ref,
                 wl_ref, e256_ref, o_ref, s_ref, last_ref, lastl_ref):
    c = pl.program_id(1)
    is_meta = c == 0

    @pl.when(is_meta)
    def _():
        s_ref[...] = jnp.zeros_like(s_ref)
        last_ref[...] = jnp.zeros_like(last_ref)
        lastl_ref[...] = jnp.zeros_like(lastl_ref)

    def pick(meta_ref, main_ref):
        return jnp.where(is_meta, meta_ref[...].astype(F32), main_ref[...].astype(F32))

    tok = lax.broadcasted_iota(jnp.int32, (RW_TOK, 1), 0)

    def shift(cur, last_row, mu):
        prev = jnp.where(tok == 0, last_row, pltpu.roll(cur, 1, 0))
        return cur + (prev - cur) * mu

    r_raw = pick(pr_ref, r_ref)
    k_raw = pick(pk_ref, k_ref)
    v_raw = pick(pv_ref, v_ref)
    lo_raw = pick(plo_ref, lo_ref)
    r = shift(r_raw, last_ref[0:1, :], mu_ref[0:1, :])
    k = shift(k_raw, last_ref[1:2, :], mu_ref[1:2, :])
    v = shift(v_raw, last_ref[2:3, :], mu_ref[2:3, :])
    lo = shift(lo_raw, lastl_ref[0:1, :], mul_ref[...])
    last_ref[0:1, :] = r_raw[RW_TOK - 1:RW_TOK, :]
    last_ref[1:2, :] = k_raw[RW_TOK - 1:RW_TOK, :]
    last_ref[2:3, :] = v_raw[RW_TOK - 1:RW_TOK, :]
    lastl_ref[0:1, :] = lo_raw[RW_TOK - 1:RW_TOK, :]

    chunk_shift = CHUNK.bit_length() - 1
    ti = lax.broadcasted_iota(jnp.int32, (CHUNK, CHUNK), 0)
    si = lax.broadcasted_iota(jnp.int32, (CHUNK, CHUNK), 1)
    tri = (si <= ti).astype(BF16)
    lane_l = lax.broadcasted_iota(jnp.int32, (CHUNK, 2 * LORA), 1)
    e256 = e256_ref[...]
    w_lora = wl_ref[...]

    row = lax.broadcasted_iota(jnp.int32, (CHUNK, 1), 0)
    lane = lax.broadcasted_iota(jnp.int32, (CHUNK, LANES), 1)
    first = lane < HEAD_DIM
    col = jnp.where(first, lane, lane - HEAD_DIM)
    strict = col < row
    incl = col <= row
    row2 = lax.broadcasted_iota(jnp.int32, (2 * CHUNK, LANES), 0)
    lane2 = lax.broadcasted_iota(jnp.int32, (2 * CHUNK, LANES), 1)
    same_head = (row2 < HEAD_DIM) == (lane2 < HEAD_DIM)
    eye2 = (row2 == lane2).astype(F32)
    pairs = range(PAIRS)

    def blockdiag(t):
        return jnp.concatenate([jnp.where(first, t, 0.0), jnp.where(first, 0.0, t)], axis=0)

    def pair(x, p):
        return x[:, LANES * p:LANES * (p + 1)]

    def prepare(ch, a):
        rows = slice(CHUNK * ch, CHUNK * (ch + 1))
        r_c, k_c, v_c, lo_c = r[rows], k[rows], v[rows], lo[rows]
        pre_w = _dot(jnp.where(lane_l < LORA, jnp.tanh(lo_c), 0.0), w_lora)
        pre_a = _dot(jnp.where(lane_l >= LORA, lo_c, 0.0), w_lora)
        kk = k_c * kk_ref[...]
        kk_sq = _head_sum(kk * kk, e256)
        yield
        logw = -DECAY_SCALE * _sigmoid(w0_ref[...] + pre_w)
        a_lr = _sigmoid(a0_ref[...] + pre_a)
        lw_hi = logw.astype(BF16)
        lw_lo = (logw - lw_hi.astype(F32)).astype(BF16)
        cw = (jnp.dot(tri, lw_hi, preferred_element_type=F32)
              + jnp.dot(tri, lw_lo, preferred_element_type=F32))
        yield
        w_in = jnp.exp(cw)
        w_ex = jnp.exp(cw - logw)
        w_inv = jnp.exp(-cw)
        kappa = kk * lax.rsqrt(kk_sq + L2_EPS)
        k_rep = k_c * (1.0 + (a_lr - 1.0) * ka_ref[...])
        kappa_h = kappa * w_ex
        r_h = r_c * w_in
        k_h = k_rep * w_inv
        b_h = kappa * a_lr * w_inv
        a.update(
            r=r_c, k_rep=k_rep, v=v_c, w_end=w_in[CHUNK - 1:CHUNK, :],
            lhs=[jnp.concatenate([pair(kappa_h, p), pair(r_h, p)], axis=0).astype(BF16)
                 for p in pairs],
            kb=[jnp.concatenate([pair(k_h, p), pair(b_h, p)], axis=0).astype(BF16) for p in pairs],
            kbd=[jnp.concatenate([blockdiag(pair(k_h, p)), blockdiag(pair(b_h, p))],
                                 axis=0).astype(BF16) for p in pairs],
            vbd=[blockdiag(pair(v_c, p)).astype(BF16) for p in pairs])

    def independent(a):
        gram = [_dot_nt(a["lhs"][p], a["kbd"][p]) for p in pairs]
        yield
        l_bd = [blockdiag(jnp.where(strict, g[:CHUNK, LANES:], 0.0)) for g in gram]
        mk = [jnp.concatenate([jnp.where(strict, g[:CHUNK, :LANES], 0.0),
                               jnp.where(incl, g[CHUNK:, :LANES], 0.0)], axis=0) for g in gram]
        a["mb_r"] = [jnp.where(incl, g[CHUNK:, LANES:], 0.0) for g in gram]
        a["mv"] = [_dot(mk[p], a["vbd"][p]) for p in pairs]
        yield
        t_inv = None
        for level in range(chunk_shift):
            rb = jnp.right_shift(row2, level)
            off = ((rb & 1) == 1) & (jnp.right_shift(lane2, level) == rb - 1)
            l_off = [jnp.where(off, l, 0.0) for l in l_bd]
            if level == 0:
                t_inv = [eye2 - l for l in l_off]
            else:
                lt = [_dot(l_off[p], t_inv[p]) for p in pairs]
                yield
                t_inv = [t_inv[p] - _dot(t_inv[p], lt[p]) for p in pairs]
                yield
        a["t_inv"] = t_inv

    state = [s_ref[p] for p in pairs]

    def dependent(a):
        sp = [_dot_nt(a["lhs"][p], state[p]) for p in pairs]
        yield
        u_bd = [_dot(a["t_inv"][p], blockdiag(sp[p][:CHUNK] + a["mv"][p][:CHUNK])) for p in pairs]
        yield
        vu_t = [jnp.concatenate([pair(a["v"], p), -(u_bd[p][:CHUNK] + u_bd[p][CHUNK:])], axis=0).T
                for p in pairs]
        ds = [_dot(vu_t[p], a["kb"][p]) for p in pairs]
        yd = [_dot(a["mb_r"][p], u_bd[p]) for p in pairs]
        yield
        for p in pairs:
            state[p] = (state[p] + jnp.where(same_head, ds[p], 0.0)) * pair(a["w_end"], p)
        a["y"] = jnp.concatenate(
            [sp[p][CHUNK:] + a["mv"][p][CHUNK:] - yd[p] for p in pairs], axis=1)

    def finish(ch, a):
        y = a["y"]
        mean = _head_sum(y, e256) * (1.0 / HEAD_DIM)
        yield
        d = y - mean
        var = _head_sum(d * d, e256) * (1.0 / HEAD_DIM)
        yield
        y = d * lax.rsqrt(var + GN_EPS) * gng_ref[...] + gnb_ref[...]
        bonus = _head_sum(a["r"] * a["k_rep"] * rk_ref[...], e256) * a["v"]
        yield
        rows = slice(CHUNK * ch, CHUNK * (ch + 1))
        g = jnp.where(is_meta, 0.0, gr_ref[rows, :].astype(F32))
        o_ref[rows, :] = ((y + bonus) * (g * _sigmoid(g))).astype(o_ref.dtype)

    chunks = [dict() for _ in range(RW_CHUNKS)]
    for ch in range(min(2, RW_CHUNKS)):
        for _ in prepare(ch, chunks[ch]):
            pass
    to_prepare = list(range(2, RW_CHUNKS))
    waiting = list(range(RW_CHUNKS))
    prep, running, finished, dep, dep_ch, tails = None, [], set(), None, 0, []
    while waiting or running or dep is not None or dep_ch < RW_CHUNKS or tails:
        if prep is None and to_prepare:
            prep = prepare(to_prepare[0], chunks[to_prepare[0]])
        while waiting and len(running) < 2 and "lhs" in chunks[waiting[0]]:
            ch = waiting.pop(0)
            running.append((ch, independent(chunks[ch])))
        if dep is None and dep_ch in finished:
            dep = dependent(chunks[dep_ch])
        for item in list(running):
            if next(item[1], "done") == "done":
                running.remove(item)
                finished.add(item[0])
        if prep is not None and next(prep, "done") == "done":
            prep = None
            to_prepare.pop(0)
        if dep is not None and next(dep, "done") == "done":
            tails.append(finish(dep_ch, chunks[dep_ch]))
            dep, dep_ch = None, dep_ch + 1
        for gen in list(tails):
            if next(gen, "done") == "done":
                tails.remove(gen)
    for p in pairs:
        s_ref[p] = state[p]


def _rwkv(p_main, lora_main, p_meta, lora_meta, rows_vec, mu_lora, w_lora, e256, batch, seq):
    per_b = seq // RW_TOK

    def main_map(seg):
        return lambda b, c: (b * per_b + jnp.maximum(c - 1, 0), seg)

    def meta_map(seg):
        return lambda b, c: (0, seg)

    row_spec = pl.BlockSpec((1, WIDTH), lambda b, c: (0, 0))
    in_specs = [
        pl.BlockSpec((RW_TOK, WIDTH), main_map(SEG_R)),
        pl.BlockSpec((RW_TOK, WIDTH), main_map(SEG_K)),
        pl.BlockSpec((RW_TOK, WIDTH), main_map(SEG_V)),
        pl.BlockSpec((RW_TOK, WIDTH), main_map(SEG_GR)),
        pl.BlockSpec((RW_TOK, 2 * LORA), main_map(0)),
        pl.BlockSpec((RW_TOK, WIDTH), meta_map(SEG_R)),
        pl.BlockSpec((RW_TOK, WIDTH), meta_map(SEG_K)),
        pl.BlockSpec((RW_TOK, WIDTH), meta_map(SEG_V)),
        pl.BlockSpec((RW_TOK, 2 * LORA), meta_map(0)),
        pl.BlockSpec((3, WIDTH), lambda b, c: (0, 0)),
        pl.BlockSpec((1, 2 * LORA), lambda b, c: (0, 0)),
    ] + [row_spec] * 7 + [
        pl.BlockSpec((2 * LORA, WIDTH), lambda b, c: (0, 0)),
        pl.BlockSpec((256, 256), lambda b, c: (0, 0)),
    ]
    mu3, w0, a0, k_k, k_a, r_k, gn_g, gn_b = rows_vec
    return pl.pallas_call(
        _rwkv_kernel,
        grid=(batch, per_b + 1),
        in_specs=in_specs,
        out_specs=pl.BlockSpec((RW_TOK, WIDTH), main_map(0)),
        out_shape=jax.ShapeDtypeStruct((batch * seq, WIDTH), BF16),
        scratch_shapes=[
            pltpu.VMEM((PAIRS, LANES, LANES), F32),
            pltpu.VMEM((8, WIDTH), F32),
            pltpu.VMEM((8, 2 * LORA), F32),
        ],
        compiler_params=pltpu.CompilerParams(
            dimension_semantics=("arbitrary", "arbitrary"), vmem_limit_bytes=VMEM_LIMIT),
        name="rwkv",
    )(p_main, p_main, p_main, p_main, lora_main, p_meta, p_meta, p_meta, lora_meta,
      mu3, mu_lora, w0, a0, k_k, k_a, r_k, gn_g, gn_b, w_lora, e256)


def _sb_kernel(q_ref, k_ref, v_ref, km_ref, vm_ref, g_ref, u_ref, o_ref, *, tq):
    i = pl.program_id(2)
    n_sub = tq // SUB
    u = u_ref[...]
    first = lax.broadcasted_iota(jnp.int32, (SUB, LANES), 1) < HEAD_DIM
    row = lax.broadcasted_iota(jnp.int32, (2 * SUB, SUB), 0)
    lane_k = lax.broadcasted_iota(jnp.int32, (2 * SUB, SUB), 1)
    causal = lane_k < jnp.where(row < SUB, row, row - SUB)
    meta_visible = lax.broadcasted_iota(jnp.int32, (2 * SUB, LANES), 1) < N_META

    def stacked_q(s):
        q = q_ref[SUB * s:SUB * (s + 1), :].astype(F32) * (HEAD_DIM ** -0.5)
        return jnp.concatenate([jnp.where(first, q, 0.0), jnp.where(first, 0.0, q)],
                               axis=0).astype(BF16)

    def cores(items):
        zs = [_dot_nt(qs, kb) for qs, kb, _, _ in items]
        sps = []
        for z, (_, _, _, visible) in zip(zs, items):
            sp = jnp.maximum(z, 0.0) + jnp.log(1.0 + jnp.exp2(jnp.abs(z) * (-LOG2E)))
            sps.append(sp if visible is None else jnp.where(visible, sp, 0.0))
        css = [jnp.dot(sp.astype(BF16), u[:sp.shape[1], :sp.shape[1]], preferred_element_type=F32)
               for sp in sps]
        out = []
        for z, sp, cs, (_, _, vb, visible) in zip(zs, sps, css, items):
            w = jnp.exp(z - sp - cs)
            if visible is not None:
                w = jnp.where(visible, w, 0.0)
            wcat = jnp.concatenate([w[:SUB], w[SUB:]], axis=1).astype(BF16)
            first_v = lax.broadcasted_iota(jnp.int32, vb.shape, 1) < HEAD_DIM
            vbd = jnp.concatenate([jnp.where(first_v, vb, 0), jnp.where(first_v, 0, vb)], axis=0)
            out.append((jnp.dot(wcat, vbd, preferred_element_type=F32), cs[:, 0:1] + sp[:, 0:1]))
        return out

    def nearer(carry):
        f = jnp.exp(-carry)
        return jnp.where(first, f[:SUB], f[SUB:])

    def keys(j):
        start = pl.multiple_of(j * SUB, SUB)
        return k_ref[pl.ds(start, SUB), :], v_ref[pl.ds(start, SUB), :]

    qss = [stacked_q(s) for s in range(n_sub)]
    items = []
    for s in range(n_sub):
        ii = i * n_sub + s
        items.append((qss[s],) + keys(ii) + (causal,))
        items.append((qss[s],) + keys(jnp.maximum(ii - 1, 0)) + (None,))
    first_two = cores(items)

    for s in range(n_sub):
        ii = i * n_sub + s
        (pv_d, tot_d), (pv_p, tot_p) = first_two[2 * s], first_two[2 * s + 1]
        has_prev = (ii > 0).astype(F32)
        acc = pv_d + pv_p * (nearer(tot_d) * has_prev)
        carry = tot_d + tot_p * has_prev

        def body(state, s=s):
            j, _, carry, acc = state
            (pv, tot), = cores([(qss[s],) + keys(j) + (None,)])
            carry_new = carry + tot
            return j - 1, jnp.min(carry_new) <= DEAD_CARRY, carry_new, acc + pv * nearer(carry)

        _, alive, carry, acc = lax.while_loop(
            lambda state: (state[0] >= 0) & state[1], body,
            (ii - 2, jnp.min(carry) <= DEAD_CARRY, carry, acc))

        def with_meta(carry, acc, s=s):
            (pv, _), = cores([(qss[s], km_ref[...], vm_ref[...], meta_visible)])
            return acc + pv * nearer(carry)

        acc = lax.cond(alive, with_meta, lambda carry, acc: acc, carry, acc)
        g = g_ref[SUB * s:SUB * (s + 1), :].astype(F32)
        o_ref[SUB * s:SUB * (s + 1), :] = (acc * (g * _sigmoid(g))).astype(o_ref.dtype)


def _sb_attn(p_main, k_meta, v_meta, u, batch, seq, tq):
    nq = seq // tq

    def seq_map(seg):
        return lambda b, p, i: (b, seg * PAIRS + p)

    def tile_map(seg):
        return lambda b, p, i: (b * nq + i, seg * PAIRS + p)

    return pl.pallas_call(
        functools.partial(_sb_kernel, tq=tq),
        grid=(batch, PAIRS, nq),
        in_specs=[
            pl.BlockSpec((tq, LANES), tile_map(SEG_Q)),
            pl.BlockSpec((seq, LANES), seq_map(SEG_KS)),
            pl.BlockSpec((seq, LANES), seq_map(SEG_VS)),
            pl.BlockSpec((LANES, LANES), lambda b, p, i: (0, p)),
            pl.BlockSpec((LANES, LANES), lambda b, p, i: (0, p)),
            pl.BlockSpec((tq, LANES), tile_map(SEG_GS)),
            pl.BlockSpec((SUB, SUB), lambda b, p, i: (0, 0)),
        ],
        out_specs=pl.BlockSpec((tq, LANES), lambda b, p, i: (b * nq + i, p)),
        out_shape=jax.ShapeDtypeStruct((batch * seq, WIDTH), BF16),
        compiler_params=pltpu.CompilerParams(
            dimension_semantics=("arbitrary", "arbitrary", "arbitrary"),
            vmem_limit_bytes=VMEM_LIMIT),
        name="sb_attn",
    )(p_main, p_main, p_main, k_meta, v_meta, p_main, u)


def _out_kernel(yr_ref, ys_ref, mr_ref, ms_ref, x_ref, wr_ref, ws_ref, wo_ref, g_ref, o_ref):
    pr = jnp.dot(yr_ref[...], wr_ref[...], preferred_element_type=F32)
    ps = jnp.dot(ys_ref[...], ws_ref[...], preferred_element_type=F32)
    mixed = (_sigmoid(mr_ref[...].astype(F32)) * pr + _sigmoid(ms_ref[...].astype(F32)) * ps)
    o = jnp.dot(mixed.astype(BF16), wo_ref[...], preferred_element_type=F32)
    inv = lax.rsqrt(jnp.mean(o * o, axis=-1, keepdims=True) + RMS_EPS)
    o_ref[...] = x_ref[...] + o * inv * g_ref[...]


def _out_proj(y_rwkv, y_sb, p_main, x2d, w_r, w_s, w_o, g, tm):
    rows = x2d.shape[0]
    w_spec = pl.BlockSpec((WIDTH, D_MODEL), lambda i: (0, 0))
    return pl.pallas_call(
        _out_kernel,
        grid=(rows // tm,),
        in_specs=[
            pl.BlockSpec((tm, WIDTH), lambda i: (i, 0)),
            pl.BlockSpec((tm, WIDTH), lambda i: (i, 0)),
            pl.BlockSpec((tm, WIDTH), lambda i: (i, SEG_MR)),
            pl.BlockSpec((tm, WIDTH), lambda i: (i, SEG_MS)),
            pl.BlockSpec((tm, D_MODEL), lambda i: (i, 0)),
            w_spec, w_spec, w_spec,
            pl.BlockSpec((1, D_MODEL), lambda i: (0, 0)),
        ],
        out_specs=pl.BlockSpec((tm, D_MODEL), lambda i: (i, 0)),
        out_shape=jax.ShapeDtypeStruct((rows, D_MODEL), F32),
        compiler_params=pltpu.CompilerParams(
            dimension_semantics=("arbitrary",), vmem_limit_bytes=VMEM_LIMIT),
        name="out_proj",
    )(y_rwkv, y_sb, p_main, p_main, x2d, w_r, w_s, w_o, g)


def _tiles(batch, seq):
    rows = batch * seq
    tm_in = min(1024, rows)
    tm_out = min(512, rows)
    tq = min(1024, seq)
    return tm_in, 2048, tm_out, tq


def kernel(x, meta_tokens, pre_norm_g, post_norm_g, w_in, rwkv_mu, rwkv_w0, rwkv_w_up, rwkv_a0,
           rwkv_a_up, rwkv_k_k, rwkv_k_a, rwkv_r_k, rwkv_gn_g, rwkv_gn_b, w_proj_rwkv, w_proj_sb,
           w_out):
    batch, seq, d_model = x.shape
    assert d_model == D_MODEL and w_in.shape[0] == 1, "single-layer kernel"
    assert seq % 512 == 0 and meta_tokens.shape == (N_META, D_MODEL)
    tm_in, tn_in, tm_out, tq = _tiles(batch, seq)

    w = w_in[0]
    w_main = jnp.concatenate([w[:, :3 * WIDTH], w[:, 3 * WIDTH + 2 * LORA:]], axis=1).astype(BF16)
    w_lo = w[:, 3 * WIDTH:3 * WIDTH + 2 * LORA].astype(BF16)
    g_pre = pre_norm_g[0][None, :]

    x2d = x.reshape(batch * seq, D_MODEL)
    p_main, lora_main = _in_proj(x2d, g_pre, w_main, w_lo, tm_in, tn_in)
    meta_rows = jnp.zeros((RW_TOK, D_MODEL), F32).at[RW_TOK - N_META:].set(meta_tokens.astype(F32))
    p_meta, lora_meta = _in_proj(meta_rows, g_pre, w_main, w_lo, RW_TOK, tn_in)

    mu = rwkv_mu[0]
    mu3 = mu[:3 * WIDTH].reshape(3, WIDTH)
    mu_lora = mu[3 * WIDTH:][None, :]
    as_row = lambda t: t[0].reshape(1, WIDTH)
    rows_vec = (mu3, as_row(rwkv_w0), as_row(rwkv_a0), as_row(rwkv_k_k), as_row(rwkv_k_a),
                as_row(rwkv_r_k), as_row(rwkv_gn_g), as_row(rwkv_gn_b))
    w_lora_up = jnp.concatenate([rwkv_w_up[0], rwkv_a_up[0]], axis=0).astype(BF16)
    idx = jnp.arange(256)
    e256 = (idx[:, None] // HEAD_DIM == idx[None, :] // HEAD_DIM).astype(BF16)
    y_rwkv = _rwkv(p_main, lora_main, p_meta, lora_meta, rows_vec, mu_lora, w_lora_up, e256,
                   batch, seq)

    kidx = jnp.arange(SUB)
    u_tri = (kidx[:, None] > kidx[None, :]).astype(BF16)
    pad = jnp.zeros((LANES - N_META, WIDTH), BF16)
    meta_seg = lambda seg: jnp.concatenate(
        [p_meta[RW_TOK - N_META:, seg * WIDTH:(seg + 1) * WIDTH], pad], axis=0)
    y_sb = _sb_attn(p_main, meta_seg(SEG_KS), meta_seg(SEG_VS), u_tri, batch, seq, tq)

    out = _out_proj(y_rwkv, y_sb, p_main, x2d, w_proj_rwkv[0].astype(BF16),
                    w_proj_sb[0].astype(BF16), w_out[0].astype(BF16), post_norm_g[0][None, :],
                    tm_out)
    return out.reshape(batch, seq, D_MODEL)
```

```python
import functools

import jax
import jax.numpy as jnp
from jax import lax
from jax.experimental import pallas as pl
from jax.experimental.pallas import tpu as pltpu

F32 = jnp.float32
BF16 = jnp.bfloat16

D_MODEL = 1024
N_META = 16
HEADS = 16
HEAD_DIM = 64
WIDTH = HEADS * HEAD_DIM
LORA = 64
RMS_EPS = 1e-6
GN_EPS = 64e-5
L2_EPS = 1e-12
DECAY_SCALE = 0.6065306597126334
LOG2E = 1.4426950408889634

LANES = 128
PAIRS = WIDTH // LANES
CHUNK = 64
RW_CHUNKS = 4
RW_TOK = RW_CHUNKS * CHUNK
SUB = 256
DEAD_CARRY = 110.0
N_SEG = 10
SEG_R, SEG_K, SEG_V, SEG_GR, SEG_Q, SEG_KS, SEG_VS, SEG_GS, SEG_MR, SEG_MS = range(N_SEG)
VMEM_LIMIT = 56 * 1024 * 1024


def _dot(a, b):
    return jnp.dot(a.astype(BF16), b.astype(BF16), preferred_element_type=F32)


def _dot_nt(a, b):
    return lax.dot_general(a.astype(BF16), b.astype(BF16), (((1,), (1,)), ((), ())),
                           preferred_element_type=F32)


def _sigmoid(x):
    return 1.0 / (1.0 + jnp.exp(-x))


def _in_proj_kernel(x_ref, g_ref, w_ref, wl_ref, p_ref, lora_ref, hn_ref):
    @pl.when(pl.program_id(1) == 0)
    def _():
        x = x_ref[...]
        inv = lax.rsqrt(jnp.mean(x * x, axis=-1, keepdims=True) + RMS_EPS)
        hn = (x * inv * g_ref[...]).astype(BF16)
        hn_ref[...] = hn
        lora_ref[...] = jnp.dot(hn, wl_ref[...], preferred_element_type=F32)

    p_ref[...] = jnp.dot(hn_ref[...], w_ref[...], preferred_element_type=F32).astype(p_ref.dtype)


def _in_proj(x2d, g, w_main, w_lora, tm, tn):
    rows = x2d.shape[0]
    cols = w_main.shape[1]
    return pl.pallas_call(
        _in_proj_kernel,
        grid=(rows // tm, cols // tn),
        in_specs=[
            pl.BlockSpec((tm, D_MODEL), lambda i, j: (i, 0)),
            pl.BlockSpec((1, D_MODEL), lambda i, j: (0, 0)),
            pl.BlockSpec((D_MODEL, tn), lambda i, j: (0, j)),
            pl.BlockSpec((D_MODEL, 2 * LORA), lambda i, j: (0, 0)),
        ],
        out_specs=[
            pl.BlockSpec((tm, tn), lambda i, j: (i, j)),
            pl.BlockSpec((tm, 2 * LORA), lambda i, j: (i, 0)),
        ],
        out_shape=[
            jax.ShapeDtypeStruct((rows, cols), BF16),
            jax.ShapeDtypeStruct((rows, 2 * LORA), F32),
        ],
        scratch_shapes=[pltpu.VMEM((tm, D_MODEL), BF16)],
        compiler_params=pltpu.CompilerParams(
            dimension_semantics=("arbitrary", "arbitrary"), vmem_limit_bytes=VMEM_LIMIT),
        name="in_proj",
    )(x2d, g, w_main, w_lora)


def _head_sum(x, e256):
    rows = x.shape[0]
    groups = WIDTH // 256
    stacked = jnp.concatenate([x[:, 256 * g:256 * (g + 1)] for g in range(groups)], axis=0)
    s = _dot(stacked, e256)
    return jnp.concatenate([s[rows * g:rows * (g + 1), :] for g in range(groups)], axis=1)


def _rwkv_kernel(r_ref, k_ref, v_ref, gr_ref, lo_ref, pr_ref, pk_ref, pv_ref, plo_ref,
                 mu_ref, mul_ref, w0_ref, a0_ref, kk_ref, ka_ref, rk_ref, gng_ref, gnb_ref,
                 wl_ref, e256_ref, o_ref, s_ref, last_ref, lastl_ref):
    c = pl.program_id(1)
    is_meta = c == 0

    @pl.when(is_meta)
    def _():
        s_ref[...] = jnp.zeros_like(s_ref)
        last_ref[...] = jnp.zeros_like(last_ref)
        lastl_ref[...] = jnp.zeros_like(lastl_ref)

    def pick(meta_ref, main_ref):
        return jnp.where(is_meta, meta_ref[...].astype(F32), main_ref[...].astype(F32))

    tok = lax.broadcasted_iota(jnp.int32, (RW_TOK, 1), 0)

    def shift(cur, last_row, mu):
        prev = jnp.where(tok == 0, last_row, pltpu.roll(cur, 1, 0))
        return cur + (prev - cur) * mu

    r_raw = pick(pr_ref, r_ref)
    k_raw = pick(pk_ref, k_ref)
    v_raw = pick(pv_ref, v_ref)
    lo_raw = pick(plo_ref, lo_ref)
    r = shift(r_raw, last_ref[0:1, :], mu_ref[0:1, :])
    k = shift(k_raw, last_ref[1:2, :], mu_ref[1:2, :])
    v = shift(v_raw, last_ref[2:3, :], mu_ref[2:3, :])
    lo = shift(lo_raw, lastl_ref[0:1, :], mul_ref[...])
    last_ref[0:1, :] = r_raw[RW_TOK - 1:RW_TOK, :]
    last_ref[1:2, :] = k_raw[RW_TOK - 1:RW_TOK, :]
    last_ref[2:3, :] = v_raw[RW_TOK - 1:RW_TOK, :]
    lastl_ref[0:1, :] = lo_raw[RW_TOK - 1:RW_TOK, :]

    chunk_shift = CHUNK.bit_length() - 1
    ti = lax.broadcasted_iota(jnp.int32, (CHUNK, CHUNK), 0)
    si = lax.broadcasted_iota(jnp.int32, (CHUNK, CHUNK), 1)
    tri = (si <= ti).astype(BF16)
    lane_l = lax.broadcasted_iota(jnp.int32, (CHUNK, 2 * LORA), 1)
    e256 = e256_ref[...]
    w_lora = wl_ref[...]

    row = lax.broadcasted_iota(jnp.int32, (CHUNK, 1), 0)
    lane = lax.broadcasted_iota(jnp.int32, (CHUNK, LANES), 1)
    first = lane < HEAD_DIM
    col = jnp.where(first, lane, lane - HEAD_DIM)
    strict = col < row
    incl = col <= row
    row2 = lax.broadcasted_iota(jnp.int32, (2 * CHUNK, LANES), 0)
    lane2 = lax.broadcasted_iota(jnp.int32, (2 * CHUNK, LANES), 1)
    same_head = (row2 < HEAD_DIM) == (lane2 < HEAD_DIM)
    eye2 = (row2 == lane2).astype(F32)
    pairs = range(PAIRS)

    def blockdiag(t):
        return jnp.concatenate([jnp.where(first, t, 0.0), jnp.where(first, 0.0, t)], axis=0)

    def pair(x, p):
        return x[:, LANES * p:LANES * (p + 1)]

    def prepare(ch, a):
        rows = slice(CHUNK * ch, CHUNK * (ch + 1))
        r_c, k_c, v_c, lo_c = r[rows], k[rows], v[rows], lo[rows]
        pre_w = _dot(jnp.where(lane_l < LORA, jnp.tanh(lo_c), 0.0), w_lora)
        pre_a = _dot(jnp.where(lane_l >= LORA, lo_c, 0.0), w_lora)
        kk = k_c * kk_ref[...]
        kk_sq = _head_sum(kk * kk, e256)
        yield
        logw = -DECAY_SCALE * _sigmoid(w0_ref[...] + pre_w)
        a_lr = _sigmoid(a0_ref[...] + pre_a)
        lw_hi = logw.astype(BF16)
        lw_lo = (logw - lw_hi.astype(F32)).astype(BF16)
        cw = jnp.dot(jnp.concatenate([tri, tri], axis=1), jnp.concatenate([lw_hi, lw_lo], axis=0),
                     preferred_element_type=F32)
        yield
        w_in = jnp.exp(cw)
        w_ex = jnp.exp(cw - logw)
        w_inv = jnp.exp(-cw)
        kappa = kk * lax.rsqrt(kk_sq + L2_EPS)
        k_rep = k_c * (1.0 + (a_lr - 1.0) * ka_ref[...])
        kappa_h = kappa * w_ex
        r_h = r_c * w_in
        k_h = k_rep * w_inv
        b_h = kappa * a_lr * w_inv
        a.update(
            r=r_c, k_rep=k_rep, v=v_c, w_end=w_in[CHUNK - 1:CHUNK, :],
            lhs=[jnp.concatenate([pair(kappa_h, p), pair(r_h, p)], axis=0).astype(BF16)
                 for p in pairs],
            kb=[jnp.concatenate([pair(k_h, p), pair(b_h, p)], axis=0).astype(BF16) for p in pairs],
            kbd=[jnp.concatenate([blockdiag(pair(k_h, p)), blockdiag(pair(b_h, p))],
                                 axis=0).astype(BF16) for p in pairs],
            vbd=[blockdiag(pair(v_c, p)).astype(BF16) for p in pairs])

    def independent(a):
        gram = [_dot_nt(a["lhs"][p], a["kbd"][p]) for p in pairs]
        yield
        l_bd = [blockdiag(jnp.where(strict, g[:CHUNK, LANES:], 0.0)) for g in gram]
        mk = [jnp.concatenate([jnp.where(strict, g[:CHUNK, :LANES], 0.0),
                               jnp.where(incl, g[CHUNK:, :LANES], 0.0)], axis=0) for g in gram]
        a["mb_r"] = [jnp.where(incl, g[CHUNK:, LANES:], 0.0) for g in gram]
        a["mv"] = [_dot(mk[p], a["vbd"][p]) for p in pairs]
        yield
        t_inv = None
        for level in range(chunk_shift):
            size = 1 << level
            rb = jnp.right_shift(row2, level)
            off = ((rb & 1) == 1) & (jnp.right_shift(lane2, level) == rb - 1)
            l_off = [jnp.where(off, l, 0.0) for l in l_bd]
            if level == 0:
                t_inv = [eye2 - l for l in l_off]
            elif size < 8:
                lt = [_dot(l_off[p], t_inv[p]) for p in pairs]
                yield
                t_inv = [t_inv[p] - _dot(t_inv[p], lt[p]) for p in pairs]
                yield
            else:
                odd = [slice(start, start + size) for start in range(size, 2 * CHUNK, 2 * size)]

                def take(x):
                    return jnp.concatenate([x[rows] for rows in odd], axis=0)

                def spread(x_odd, x_even):
                    pieces = []
                    for n, rows in enumerate(odd):
                        pieces += [x_even[rows.start - size:rows.start], x_odd[size * n:size * (n + 1)]]
                    return jnp.concatenate(pieces, axis=0)

                lt_odd = [_dot(take(l_off[p]), t_inv[p]) for p in pairs]
                yield
                t_odd = [take(t_inv[p]) for p in pairs]
                zero = jnp.zeros((2 * CHUNK, LANES), F32)
                delta = [_dot(t_odd[p], spread(lt_odd[p], zero)) for p in pairs]
                yield
                t_inv = [spread(t_odd[p] - delta[p], t_inv[p]) for p in pairs]
        a["t_inv"] = t_inv

    state = [s_ref[p] for p in pairs]

    def dependent(a):
        sp = [_dot_nt(a["lhs"][p], state[p]) for p in pairs]
        yield
        u_bd = [_dot(a["t_inv"][p], blockdiag(sp[p][:CHUNK] + a["mv"][p][:CHUNK])) for p in pairs]
        yield
        vu_t = [jnp.concatenate([pair(a["v"], p), -(u_bd[p][:CHUNK] + u_bd[p][CHUNK:])], axis=0).T
                for p in pairs]
        ds = [_dot(vu_t[p], a["kb"][p]) for p in pairs]
        yd = [_dot(a["mb_r"][p], u_bd[p]) for p in pairs]
        yield
        for p in pairs:
            state[p] = (state[p] + jnp.where(same_head, ds[p], 0.0)) * pair(a["w_end"], p)
        a["y"] = jnp.concatenate(
            [sp[p][CHUNK:] + a["mv"][p][CHUNK:] - yd[p] for p in pairs], axis=1)

    def finish(ch, a):
        y = a["y"]
        mean = _head_sum(y, e256) * (1.0 / HEAD_DIM)
        yield
        d = y - mean
        var = _head_sum(d * d, e256) * (1.0 / HEAD_DIM)
        yield
        y = d * lax.rsqrt(var + GN_EPS) * gng_ref[...] + gnb_ref[...]
        bonus = _head_sum(a["r"] * a["k_rep"] * rk_ref[...], e256) * a["v"]
        yield
        rows = slice(CHUNK * ch, CHUNK * (ch + 1))
        g = jnp.where(is_meta, 0.0, gr_ref[rows, :].astype(F32))
        o_ref[rows, :] = ((y + bonus) * (g * _sigmoid(g))).astype(o_ref.dtype)

    chunks = [dict() for _ in range(RW_CHUNKS)]
    for ch in range(min(2, RW_CHUNKS)):
        for _ in prepare(ch, chunks[ch]):
            pass
    to_prepare = list(range(2, RW_CHUNKS))
    waiting = list(range(RW_CHUNKS))
    prep, running, finished, dep, dep_ch, tails = None, [], set(), None, 0, []
    while waiting or running or dep is not None or dep_ch < RW_CHUNKS or tails:
        if prep is None and to_prepare:
            prep = prepare(to_prepare[0], chunks[to_prepare[0]])
        while waiting and len(running) < 2 and "lhs" in chunks[waiting[0]]:
            ch = waiting.pop(0)
            running.append((ch, independent(chunks[ch])))
        if dep is None and dep_ch in finished:
            dep = dependent(chunks[dep_ch])
        for item in list(running):
            if next(item[1], "done") == "done":
                running.remove(item)
                finished.add(item[0])
        if prep is not None and next(prep, "done") == "done":
            prep = None
            to_prepare.pop(0)
        if dep is not None and next(dep, "done") == "done":
            tails.append(finish(dep_ch, chunks[dep_ch]))
            dep, dep_ch = None, dep_ch + 1
        for gen in list(tails):
            if next(gen, "done") == "done":
                tails.remove(gen)
    for p in pairs:
        s_ref[p] = state[p]


def _rwkv(p_main, lora_main, p_meta, lora_meta, rows_vec, mu_lora, w_lora, e256, batch, seq):
    per_b = seq // RW_TOK

    def main_map(seg):
        return lambda b, c: (b * per_b + jnp.maximum(c - 1, 0), seg)

    def meta_map(seg):
        return lambda b, c: (0, seg)

    row_spec = pl.BlockSpec((1, WIDTH), lambda b, c: (0, 0))
    in_specs = [
        pl.BlockSpec((RW_TOK, WIDTH), main_map(SEG_R)),
        pl.BlockSpec((RW_TOK, WIDTH), main_map(SEG_K)),
        pl.BlockSpec((RW_TOK, WIDTH), main_map(SEG_V)),
        pl.BlockSpec((RW_TOK, WIDTH), main_map(SEG_GR)),
        pl.BlockSpec((RW_TOK, 2 * LORA), main_map(0)),
        pl.BlockSpec((RW_TOK, WIDTH), meta_map(SEG_R)),
        pl.BlockSpec((RW_TOK, WIDTH), meta_map(SEG_K)),
        pl.BlockSpec((RW_TOK, WIDTH), meta_map(SEG_V)),
        pl.BlockSpec((RW_TOK, 2 * LORA), meta_map(0)),
        pl.BlockSpec((3, WIDTH), lambda b, c: (0, 0)),
        pl.BlockSpec((1, 2 * LORA), lambda b, c: (0, 0)),
    ] + [row_spec] * 7 + [
        pl.BlockSpec((2 * LORA, WIDTH), lambda b, c: (0, 0)),
        pl.BlockSpec((256, 256), lambda b, c: (0, 0)),
    ]
    mu3, w0, a0, k_k, k_a, r_k, gn_g, gn_b = rows_vec
    return pl.pallas_call(
        _rwkv_kernel,
        grid=(batch, per_b + 1),
        in_specs=in_specs,
        out_specs=pl.BlockSpec((RW_TOK, WIDTH), main_map(0)),
        out_shape=jax.ShapeDtypeStruct((batch * seq, WIDTH), BF16),
        scratch_shapes=[
            pltpu.VMEM((PAIRS, LANES, LANES), F32),
            pltpu.VMEM((8, WIDTH), F32),
            pltpu.VMEM((8, 2 * LORA), F32),
        ],
        compiler_params=pltpu.CompilerParams(
            dimension_semantics=("arbitrary", "arbitrary"), vmem_limit_bytes=VMEM_LIMIT),
        name="rwkv",
    )(p_main, p_main, p_main, p_main, lora_main, p_meta, p_meta, p_meta, lora_meta,
      mu3, mu_lora, w0, a0, k_k, k_a, r_k, gn_g, gn_b, w_lora, e256)


def _sb_kernel(q_ref, k_ref, v_ref, km_ref, vm_ref, g_ref, u_ref, o_ref, *, tq):
    i = pl.program_id(2)
    n_sub = tq // SUB
    u = u_ref[...]
    first = lax.broadcasted_iota(jnp.int32, (SUB, LANES), 1) < HEAD_DIM
    row = lax.broadcasted_iota(jnp.int32, (2 * SUB, SUB), 0)
    lane_k = lax.broadcasted_iota(jnp.int32, (2 * SUB, SUB), 1)
    causal = lane_k < jnp.where(row < SUB, row, row - SUB)
    meta_visible = lax.broadcasted_iota(jnp.int32, (2 * SUB, LANES), 1) < N_META

    def stacked_q(s):
        q = q_ref[SUB * s:SUB * (s + 1), :].astype(F32) * (HEAD_DIM ** -0.5)
        return jnp.concatenate([jnp.where(first, q, 0.0), jnp.where(first, 0.0, q)],
                               axis=0).astype(BF16)

    def cores(items):
        zs = [_dot_nt(qs, kb) for qs, kb, _, _ in items]
        sps = []
        for z, (_, _, _, visible) in zip(zs, items):
            zb = z.astype(BF16)
            sp = jnp.maximum(zb, 0) + jnp.log(1 + jnp.exp2(jnp.abs(zb) * (-LOG2E)))
            sps.append(sp if visible is None else jnp.where(visible, sp, 0))
        css = [jnp.dot(sp, u[:sp.shape[1], :sp.shape[1]], preferred_element_type=F32)
               for sp in sps]
        out = []
        for z, sp, cs, (_, _, vb, visible) in zip(zs, sps, css, items):
            w = jnp.exp(z - sp.astype(F32) - cs)
            if visible is not None:
                w = jnp.where(visible, w, 0.0)
            wcat = jnp.concatenate([w[:SUB], w[SUB:]], axis=1).astype(BF16)
            first_v = lax.broadcasted_iota(jnp.int32, vb.shape, 1) < HEAD_DIM
            vbd = jnp.concatenate([jnp.where(first_v, vb, 0), jnp.where(first_v, 0, vb)], axis=0)
            out.append((jnp.dot(wcat, vbd, preferred_element_type=F32), cs[:, 0:1] + sp[:, 0:1]))
        return out

    def nearer(carry):
        f = jnp.exp(-carry)
        return jnp.where(first, f[:SUB], f[SUB:])

    def keys(j):
        start = pl.multiple_of(j * SUB, SUB)
        return k_ref[pl.ds(start, SUB), :], v_ref[pl.ds(start, SUB), :]

    qss = [stacked_q(s) for s in range(n_sub)]
    items = []
    for s in range(n_sub):
        ii = i * n_sub + s
        items.append((qss[s],) + keys(ii) + (causal,))
        items.append((qss[s],) + keys(jnp.maximum(ii - 1, 0)) + (None,))
    first_two = cores(items)

    carries, accs = [], []
    for s in range(n_sub):
        (pv_d, tot_d), (pv_p, tot_p) = first_two[2 * s], first_two[2 * s + 1]
        has_prev = (i * n_sub + s > 0).astype(F32)
        accs.append(pv_d + pv_p * (nearer(tot_d) * has_prev))
        carries.append(tot_d + tot_p * has_prev)

    def alive(carry):
        return jnp.min(carry) <= DEAD_CARRY

    def farther(*operands):
        out = []
        for s in range(n_sub):
            carry, acc = operands[s], operands[n_sub + s]

            def body(state, s=s):
                j, _, carry, acc = state
                (pv, tot), = cores([(qss[s],) + keys(j) + (None,)])
                carry_new = carry + tot
                return j - 1, alive(carry_new), carry_new, acc + pv * nearer(carry)

            _, still, carry, acc = lax.while_loop(
                lambda state: (state[0] >= 0) & state[1], body,
                (i * n_sub + s - 2, alive(carry), carry, acc))

            def with_meta(carry, acc, s=s):
                (pv, _), = cores([(qss[s], km_ref[...], vm_ref[...], meta_visible)])
                return acc + pv * nearer(carry)

            out.append(lax.cond(still, with_meta, lambda carry, acc: acc, carry, acc))
        return tuple(out)

    any_alive = functools.reduce(jnp.logical_or, [alive(carry) for carry in carries])
    accs = lax.cond(any_alive, farther, lambda *operands: tuple(operands[n_sub:]), *carries, *accs)
    for s in range(n_sub):
        g = g_ref[SUB * s:SUB * (s + 1), :].astype(F32)
        o_ref[SUB * s:SUB * (s + 1), :] = (accs[s] * (g * _sigmoid(g))).astype(o_ref.dtype)


def _sb_attn(p_main, k_meta, v_meta, u, batch, seq, tq):
    nq = seq // tq

    def seq_map(seg):
        return lambda b, p, i: (b, seg * PAIRS + p)

    def tile_map(seg):
        return lambda b, p, i: (b * nq + i, seg * PAIRS + p)

    return pl.pallas_call(
        functools.partial(_sb_kernel, tq=tq),
        grid=(batch, PAIRS, nq),
        in_specs=[
            pl.BlockSpec((tq, LANES), tile_map(SEG_Q)),
            pl.BlockSpec((seq, LANES), seq_map(SEG_KS)),
            pl.BlockSpec((seq, LANES), seq_map(SEG_VS)),
            pl.BlockSpec((LANES, LANES), lambda b, p, i: (0, p)),
            pl.BlockSpec((LANES, LANES), lambda b, p, i: (0, p)),
            pl.BlockSpec((tq, LANES), tile_map(SEG_GS)),
            pl.BlockSpec((SUB, SUB), lambda b, p, i: (0, 0)),
        ],
        out_specs=pl.BlockSpec((tq, LANES), lambda b, p, i: (b * nq + i, p)),
        out_shape=jax.ShapeDtypeStruct((batch * seq, WIDTH), BF16),
        compiler_params=pltpu.CompilerParams(
            dimension_semantics=("arbitrary", "arbitrary", "arbitrary"),
            vmem_limit_bytes=VMEM_LIMIT),
        name="sb_attn",
    )(p_main, p_main, p_main, k_meta, v_meta, p_main, u)


def _out_kernel(yr_ref, ys_ref, mr_ref, ms_ref, x_ref, wr_ref, ws_ref, wo_ref, g_ref, o_ref):
    pr = jnp.dot(yr_ref[...], wr_ref[...], preferred_element_type=F32)
    ps = jnp.dot(ys_ref[...], ws_ref[...], preferred_element_type=F32)
    mixed = (_sigmoid(mr_ref[...].astype(F32)) * pr + _sigmoid(ms_ref[...].astype(F32)) * ps)
    o = jnp.dot(mixed.astype(BF16), wo_ref[...], preferred_element_type=F32)
    inv = lax.rsqrt(jnp.mean(o * o, axis=-1, keepdims=True) + RMS_EPS)
    o_ref[...] = x_ref[...] + o * inv * g_ref[...]


def _out_proj(y_rwkv, y_sb, p_main, x2d, w_r, w_s, w_o, g, tm):
    rows = x2d.shape[0]
    w_spec = pl.BlockSpec((WIDTH, D_MODEL), lambda i: (0, 0))
    return pl.pallas_call(
        _out_kernel,
        grid=(rows // tm,),
        in_specs=[
            pl.BlockSpec((tm, WIDTH), lambda i: (i, 0)),
            pl.BlockSpec((tm, WIDTH), lambda i: (i, 0)),
            pl.BlockSpec((tm, WIDTH), lambda i: (i, SEG_MR)),
            pl.BlockSpec((tm, WIDTH), lambda i: (i, SEG_MS)),
            pl.BlockSpec((tm, D_MODEL), lambda i: (i, 0)),
            w_spec, w_spec, w_spec,
            pl.BlockSpec((1, D_MODEL), lambda i: (0, 0)),
        ],
        out_specs=pl.BlockSpec((tm, D_MODEL), lambda i: (i, 0)),
        out_shape=jax.ShapeDtypeStruct((rows, D_MODEL), F32),
        compiler_params=pltpu.CompilerParams(
            dimension_semantics=("arbitrary",), vmem_limit_bytes=VMEM_LIMIT),
        name="out_proj",
    )(y_rwkv, y_sb, p_main, p_main, x2d, w_r, w_s, w_o, g)


def _tiles(batch, seq):
    rows = batch * seq
    tm_in = min(1024, rows)
    tm_out = min(512, rows)
    tq = min(1024, seq)
    return tm_in, 2048, tm_out, tq


def kernel(x, meta_tokens, pre_norm_g, post_norm_g, w_in, rwkv_mu, rwkv_w0, rwkv_w_up, rwkv_a0,
           rwkv_a_up, rwkv_k_k, rwkv_k_a, rwkv_r_k, rwkv_gn_g, rwkv_gn_b, w_proj_rwkv, w_proj_sb,
           w_out):
    batch, seq, d_model = x.shape
    assert d_model == D_MODEL and w_in.shape[0] == 1, "single-layer kernel"
    assert seq % 512 == 0 and meta_tokens.shape == (N_META, D_MODEL)
    tm_in, tn_in, tm_out, tq = _tiles(batch, seq)

    w = w_in[0]
    w_main = jnp.concatenate([w[:, :3 * WIDTH], w[:, 3 * WIDTH + 2 * LORA:]], axis=1).astype(BF16)
    w_lo = w[:, 3 * WIDTH:3 * WIDTH + 2 * LORA].astype(BF16)
    g_pre = pre_norm_g[0][None, :]

    x2d = x.reshape(batch * seq, D_MODEL)
    p_main, lora_main = _in_proj(x2d, g_pre, w_main, w_lo, tm_in, tn_in)
    meta_rows = jnp.zeros((RW_TOK, D_MODEL), F32).at[RW_TOK - N_META:].set(meta_tokens.astype(F32))
    p_meta, lora_meta = _in_proj(meta_rows, g_pre, w_main, w_lo, RW_TOK, tn_in)

    mu = rwkv_mu[0]
    mu3 = mu[:3 * WIDTH].reshape(3, WIDTH)
    mu_lora = mu[3 * WIDTH:][None, :]
    as_row = lambda t: t[0].reshape(1, WIDTH)
    rows_vec = (mu3, as_row(rwkv_w0), as_row(rwkv_a0), as_row(rwkv_k_k), as_row(rwkv_k_a),
                as_row(rwkv_r_k), as_row(rwkv_gn_g), as_row(rwkv_gn_b))
    w_lora_up = jnp.concatenate([rwkv_w_up[0], rwkv_a_up[0]], axis=0).astype(BF16)
    idx = jnp.arange(256)
    e256 = (idx[:, None] // HEAD_DIM == idx[None, :] // HEAD_DIM).astype(BF16)
    y_rwkv = _rwkv(p_main, lora_main, p_meta, lora_meta, rows_vec, mu_lora, w_lora_up, e256,
                   batch, seq)

    kidx = jnp.arange(SUB)
    u_tri = (kidx[:, None] > kidx[None, :]).astype(BF16)
    pad = jnp.zeros((LANES - N_META, WIDTH), BF16)
    meta_seg = lambda seg: jnp.concatenate(
        [p_meta[RW_TOK - N_META:, seg * WIDTH:(seg + 1) * WIDTH], pad], axis=0)
    y_sb = _sb_attn(p_main, meta_seg(SEG_KS), meta_seg(SEG_VS), u_tri, batch, seq, tq)

    out = _out_proj(y_rwkv, y_sb, p_main, x2d, w_proj_rwkv[0].astype(BF16),
                    w_proj_sb[0].astype(BF16), w_out[0].astype(BF16), post_norm_g[0][None, :],
                    tm_out)
    return out.reshape(batch, seq, D_MODEL)
```

```python
import functools

import jax
import jax.numpy as jnp
from jax import lax
from jax.experimental import pallas as pl
from jax.experimental.pallas import tpu as pltpu

F32 = jnp.float32
BF16 = jnp.bfloat16

D_MODEL = 1024
N_META = 16
HEADS = 16
HEAD_DIM = 64
WIDTH = HEADS * HEAD_DIM
LORA = 64
RMS_EPS = 1e-6
GN_EPS = 64e-5
L2_EPS = 1e-12
DECAY_SCALE = 0.6065306597126334

LANES = 128
PAIRS = WIDTH // LANES
CHUNK = 64
RW_CHUNKS = 4
RW_TOK = RW_CHUNKS * CHUNK
SUB = 256
DEAD_CARRY = 110.0
N_SEG = 10
SEG_R, SEG_K, SEG_V, SEG_GR, SEG_Q, SEG_KS, SEG_VS, SEG_GS, SEG_MR, SEG_MS = range(N_SEG)
VMEM_LIMIT = 56 * 1024 * 1024


def _dot(a, b):
    return jnp.dot(a.astype(BF16), b.astype(BF16), preferred_element_type=F32)


def _dot_nt(a, b):
    return lax.dot_general(a.astype(BF16), b.astype(BF16), (((1,), (1,)), ((), ())),
                           preferred_element_type=F32)


def _sigmoid(x):
    return 1.0 / (1.0 + jnp.exp(-x))


def _in_proj_kernel(x_ref, g_ref, w_ref, wl_ref, p_ref, lora_ref, hn_ref):
    @pl.when(pl.program_id(1) == 0)
    def _():
        x = x_ref[...]
        inv = lax.rsqrt(jnp.mean(x * x, axis=-1, keepdims=True) + RMS_EPS)
        hn = (x * inv * g_ref[...]).astype(BF16)
        hn_ref[...] = hn
        lora_ref[...] = jnp.dot(hn, wl_ref[...], preferred_element_type=F32)

    p_ref[...] = jnp.dot(hn_ref[...], w_ref[...], preferred_element_type=F32).astype(p_ref.dtype)


def _in_proj(x2d, g, w_main, w_lora, tm, tn):
    rows = x2d.shape[0]
    cols = w_main.shape[1]
    return pl.pallas_call(
        _in_proj_kernel,
        grid=(rows // tm, cols // tn),
        in_specs=[
            pl.BlockSpec((tm, D_MODEL), lambda i, j: (i, 0)),
            pl.BlockSpec((1, D_MODEL), lambda i, j: (0, 0)),
            pl.BlockSpec((D_MODEL, tn), lambda i, j: (0, j)),
            pl.BlockSpec((D_MODEL, 2 * LORA), lambda i, j: (0, 0)),
        ],
        out_specs=[
            pl.BlockSpec((tm, tn), lambda i, j: (i, j)),
            pl.BlockSpec((tm, 2 * LORA), lambda i, j: (i, 0)),
        ],
        out_shape=[
            jax.ShapeDtypeStruct((rows, cols), BF16),
            jax.ShapeDtypeStruct((rows, 2 * LORA), F32),
        ],
        scratch_shapes=[pltpu.VMEM((tm, D_MODEL), BF16)],
        compiler_params=pltpu.CompilerParams(
            dimension_semantics=("arbitrary", "arbitrary"), vmem_limit_bytes=VMEM_LIMIT),
        name="in_proj",
    )(x2d, g, w_main, w_lora)


def _head_sum(x, e256):
    rows = x.shape[0]
    groups = WIDTH // 256
    stacked = jnp.concatenate([x[:, 256 * g:256 * (g + 1)] for g in range(groups)], axis=0)
    s = _dot(stacked, e256)
    return jnp.concatenate([s[rows * g:rows * (g + 1), :] for g in range(groups)], axis=1)


def _rwkv_kernel(r_ref, k_ref, v_ref, gr_ref, lo_ref, pr_ref, pk_ref, pv_ref, plo_ref,
                 mu_ref, mul_ref, w0_ref, a0_ref, kk_ref, ka_ref, rk_ref, gng_ref, gnb_ref,
                 wl_ref, e256_ref, o_ref, s_ref, last_ref, lastl_ref):
    c = pl.program_id(1)
    is_meta = c == 0

    @pl.when(is_meta)
    def _():
        s_ref[...] = jnp.zeros_like(s_ref)
        last_ref[...] = jnp.zeros_like(last_ref)
        lastl_ref[...] = jnp.zeros_like(lastl_ref)

    def pick(meta_ref, main_ref):
        return jnp.where(is_meta, meta_ref[...].astype(F32), main_ref[...].astype(F32))

    tok = lax.broadcasted_iota(jnp.int32, (RW_TOK, 1), 0)

    def shift(cur, last_row, mu):
        prev = jnp.where(tok == 0, last_row, pltpu.roll(cur, 1, 0))
        return cur + (prev - cur) * mu

    r_raw = pick(pr_ref, r_ref)
    k_raw = pick(pk_ref, k_ref)
    v_raw = pick(pv_ref, v_ref)
    lo_raw = pick(plo_ref, lo_ref)
    r = shift(r_raw, last_ref[0:1, :], mu_ref[0:1, :])
    k = shift(k_raw, last_ref[1:2, :], mu_ref[1:2, :])
    v = shift(v_raw, last_ref[2:3, :], mu_ref[2:3, :])
    lo = shift(lo_raw, lastl_ref[0:1, :], mul_ref[...])
    last_ref[0:1, :] = r_raw[RW_TOK - 1:RW_TOK, :]
    last_ref[1:2, :] = k_raw[RW_TOK - 1:RW_TOK, :]
    last_ref[2:3, :] = v_raw[RW_TOK - 1:RW_TOK, :]
    lastl_ref[0:1, :] = lo_raw[RW_TOK - 1:RW_TOK, :]

    chunk_shift = CHUNK.bit_length() - 1
    ti = lax.broadcasted_iota(jnp.int32, (CHUNK, CHUNK), 0)
    si = lax.broadcasted_iota(jnp.int32, (CHUNK, CHUNK), 1)
    tri = (si <= ti).astype(BF16)
    lane_l = lax.broadcasted_iota(jnp.int32, (CHUNK, 2 * LORA), 1)
    e256 = e256_ref[...]
    w_lora = wl_ref[...]

    row = lax.broadcasted_iota(jnp.int32, (CHUNK, 1), 0)
    lane = lax.broadcasted_iota(jnp.int32, (CHUNK, LANES), 1)
    first = lane < HEAD_DIM
    col = jnp.where(first, lane, lane - HEAD_DIM)
    strict = col < row
    incl = col <= row
    row2 = lax.broadcasted_iota(jnp.int32, (2 * CHUNK, LANES), 0)
    lane2 = lax.broadcasted_iota(jnp.int32, (2 * CHUNK, LANES), 1)
    same_head = (row2 < HEAD_DIM) == (lane2 < HEAD_DIM)
    eye2 = (row2 == lane2).astype(F32)
    pairs = range(PAIRS)

    def blockdiag(t):
        return jnp.concatenate([jnp.where(first, t, 0.0), jnp.where(first, 0.0, t)], axis=0)

    def pair(x, p):
        return x[:, LANES * p:LANES * (p + 1)]

    def prepare(ch, a):
        rows = slice(CHUNK * ch, CHUNK * (ch + 1))
        r_c, k_c, v_c, lo_c = r[rows], k[rows], v[rows], lo[rows]
        pre_w = _dot(jnp.where(lane_l < LORA, jnp.tanh(lo_c), 0.0), w_lora)
        pre_a = _dot(jnp.where(lane_l >= LORA, lo_c, 0.0), w_lora)
        kk = k_c * kk_ref[...]
        kk_sq = _head_sum(kk * kk, e256)
        yield
        logw = -DECAY_SCALE * _sigmoid(w0_ref[...] + pre_w)
        a_lr = _sigmoid(a0_ref[...] + pre_a)
        lw_hi = logw.astype(BF16)
        lw_lo = (logw - lw_hi.astype(F32)).astype(BF16)
        cw = jnp.dot(jnp.concatenate([tri, tri], axis=1), jnp.concatenate([lw_hi, lw_lo], axis=0),
                     preferred_element_type=F32)
        yield
        w_in = jnp.exp(cw)
        w_ex = jnp.exp(cw - logw)
        w_inv = jnp.exp(-cw)
        kappa = kk * lax.rsqrt(kk_sq + L2_EPS)
        k_rep = k_c * (1.0 + (a_lr - 1.0) * ka_ref[...])
        kappa_h = kappa * w_ex
        r_h = r_c * w_in
        k_h = k_rep * w_inv
        b_h = kappa * a_lr * w_inv
        a.update(
            r=r_c, k_rep=k_rep, v=v_c, w_end=w_in[CHUNK - 1:CHUNK, :],
            lhs=[jnp.concatenate([pair(kappa_h, p), pair(r_h, p)], axis=0).astype(BF16)
                 for p in pairs],
            kb=[jnp.concatenate([pair(k_h, p), pair(b_h, p)], axis=0).astype(BF16) for p in pairs],
            kbd=[jnp.concatenate([blockdiag(pair(k_h, p)), blockdiag(pair(b_h, p))],
                                 axis=0).astype(BF16) for p in pairs],
            vbd=[blockdiag(pair(v_c, p)).astype(BF16) for p in pairs])

    def independent(a):
        gram = [_dot_nt(a["lhs"][p], a["kbd"][p]) for p in pairs]
        yield
        l_bd = [blockdiag(jnp.where(strict, g[:CHUNK, LANES:], 0.0)) for g in gram]
        mk = [jnp.concatenate([jnp.where(strict, g[:CHUNK, :LANES], 0.0),
                               jnp.where(incl, g[CHUNK:, :LANES], 0.0)], axis=0) for g in gram]
        a["mb_r"] = [jnp.where(incl, g[CHUNK:, LANES:], 0.0) for g in gram]
        a["mv"] = [_dot(mk[p], a["vbd"][p]) for p in pairs]
        yield
        t_inv = None
        for level in range(chunk_shift):
            size = 1 << level
            rb = jnp.right_shift(row2, level)
            off = ((rb & 1) == 1) & (jnp.right_shift(lane2, level) == rb - 1)
            l_off = [jnp.where(off, l, 0.0) for l in l_bd]
            if level == 0:
                t_inv = [eye2 - l for l in l_off]
            elif size < 8:
                lt = [_dot(l_off[p], t_inv[p]) for p in pairs]
                yield
                t_inv = [t_inv[p] - _dot(t_inv[p], lt[p]) for p in pairs]
                yield
            else:
                odd = [slice(start, start + size) for start in range(size, 2 * CHUNK, 2 * size)]

                def take(x):
                    return jnp.concatenate([x[rows] for rows in odd], axis=0)

                def spread(x_odd, x_even):
                    pieces = []
                    for n, rows in enumerate(odd):
                        pieces += [x_even[rows.start - size:rows.start], x_odd[size * n:size * (n + 1)]]
                    return jnp.concatenate(pieces, axis=0)

                lt_odd = [_dot(take(l_off[p]), t_inv[p]) for p in pairs]
                yield
                t_odd = [take(t_inv[p]) for p in pairs]
                zero = jnp.zeros((2 * CHUNK, LANES), F32)
                delta = [_dot(t_odd[p], spread(lt_odd[p], zero)) for p in pairs]
                yield
                t_inv = [spread(t_odd[p] - delta[p], t_inv[p]) for p in pairs]
        a["t_inv"] = t_inv

    state = [s_ref[p] for p in pairs]

    def dependent(a):
        sp = [_dot_nt(a["lhs"][p], state[p]) for p in pairs]
        yield
        u_bd = [_dot(a["t_inv"][p], blockdiag(sp[p][:CHUNK] + a["mv"][p][:CHUNK])) for p in pairs]
        yield
        vu_t = [jnp.concatenate([pair(a["v"], p), -(u_bd[p][:CHUNK] + u_bd[p][CHUNK:])], axis=0).T
                for p in pairs]
        ds = [_dot(vu_t[p], a["kb"][p]) for p in pairs]
        yd = [_dot(a["mb_r"][p], u_bd[p]) for p in pairs]
        yield
        for p in pairs:
            state[p] = (state[p] + jnp.where(same_head, ds[p], 0.0)) * pair(a["w_end"], p)
        a["y"] = jnp.concatenate(
            [sp[p][CHUNK:] + a["mv"][p][CHUNK:] - yd[p] for p in pairs], axis=1)

    def finish(ch, a):
        y = a["y"]
        mean = _head_sum(y, e256) * (1.0 / HEAD_DIM)
        yield
        d = y - mean
        var = _head_sum(d * d, e256) * (1.0 / HEAD_DIM)
        yield
        y = d * lax.rsqrt(var + GN_EPS) * gng_ref[...] + gnb_ref[...]
        bonus = _head_sum(a["r"] * a["k_rep"] * rk_ref[...], e256) * a["v"]
        yield
        rows = slice(CHUNK * ch, CHUNK * (ch + 1))
        g = jnp.where(is_meta, 0.0, gr_ref[rows, :].astype(F32))
        o_ref[rows, :] = ((y + bonus) * (g * _sigmoid(g))).astype(o_ref.dtype)

    chunks = [dict() for _ in range(RW_CHUNKS)]
    for ch in range(min(2, RW_CHUNKS)):
        for _ in prepare(ch, chunks[ch]):
            pass
    to_prepare = list(range(2, RW_CHUNKS))
    waiting = list(range(RW_CHUNKS))
    prep, running, finished, dep, dep_ch, tails = None, [], set(), None, 0, []
    while waiting or running or dep is not None or dep_ch < RW_CHUNKS or tails:
        if prep is None and to_prepare:
            prep = prepare(to_prepare[0], chunks[to_prepare[0]])
        while waiting and len(running) < 2 and "lhs" in chunks[waiting[0]]:
            ch = waiting.pop(0)
            running.append((ch, independent(chunks[ch])))
        if dep is None and dep_ch in finished:
            dep = dependent(chunks[dep_ch])
        for item in list(running):
            if next(item[1], "done") == "done":
                running.remove(item)
                finished.add(item[0])
        if prep is not None and next(prep, "done") == "done":
            prep = None
            to_prepare.pop(0)
        if dep is not None and next(dep, "done") == "done":
            tails.append(finish(dep_ch, chunks[dep_ch]))
            dep, dep_ch = None, dep_ch + 1
        for gen in list(tails):
            if next(gen, "done") == "done":
                tails.remove(gen)
    for p in pairs:
        s_ref[p] = state[p]


def _rwkv(p_main, lora_main, p_meta, lora_meta, rows_vec, mu_lora, w_lora, e256, batch, seq):
    per_b = seq // RW_TOK

    def main_map(seg):
        return lambda b, c: (b * per_b + jnp.maximum(c - 1, 0), seg)

    def meta_map(seg):
        return lambda b, c: (0, seg)

    row_spec = pl.BlockSpec((1, WIDTH), lambda b, c: (0, 0))
    in_specs = [
        pl.BlockSpec((RW_TOK, WIDTH), main_map(SEG_R)),
        pl.BlockSpec((RW_TOK, WIDTH), main_map(SEG_K)),
        pl.BlockSpec((RW_TOK, WIDTH), main_map(SEG_V)),
        pl.BlockSpec((RW_TOK, WIDTH), main_map(SEG_GR)),
        pl.BlockSpec((RW_TOK, 2 * LORA), main_map(0)),
        pl.BlockSpec((RW_TOK, WIDTH), meta_map(SEG_R)),
        pl.BlockSpec((RW_TOK, WIDTH), meta_map(SEG_K)),
        pl.BlockSpec((RW_TOK, WIDTH), meta_map(SEG_V)),
        pl.BlockSpec((RW_TOK, 2 * LORA), meta_map(0)),
        pl.BlockSpec((3, WIDTH), lambda b, c: (0, 0)),
        pl.BlockSpec((1, 2 * LORA), lambda b, c: (0, 0)),
    ] + [row_spec] * 7 + [
        pl.BlockSpec((2 * LORA, WIDTH), lambda b, c: (0, 0)),
        pl.BlockSpec((256, 256), lambda b, c: (0, 0)),
    ]
    mu3, w0, a0, k_k, k_a, r_k, gn_g, gn_b = rows_vec
    return pl.pallas_call(
        _rwkv_kernel,
        grid=(batch, per_b + 1),
        in_specs=in_specs,
        out_specs=pl.BlockSpec((RW_TOK, WIDTH), main_map(0)),
        out_shape=jax.ShapeDtypeStruct((batch * seq, WIDTH), BF16),
        scratch_shapes=[
            pltpu.VMEM((PAIRS, LANES, LANES), F32),
            pltpu.VMEM((8, WIDTH), F32),
            pltpu.VMEM((8, 2 * LORA), F32),
        ],
        compiler_params=pltpu.CompilerParams(
            dimension_semantics=("arbitrary", "arbitrary"), vmem_limit_bytes=VMEM_LIMIT),
        name="rwkv",
    )(p_main, p_main, p_main, p_main, lora_main, p_meta, p_meta, p_meta, lora_meta,
      mu3, mu_lora, w0, a0, k_k, k_a, r_k, gn_g, gn_b, w_lora, e256)


def _sb_kernel(q_ref, k_ref, v_ref, km_ref, vm_ref, g_ref, u_ref, o_ref, *, tq):
    i = pl.program_id(2)
    n_sub = tq // SUB
    u = u_ref[...]
    first = lax.broadcasted_iota(jnp.int32, (SUB, LANES), 1) < HEAD_DIM
    row = lax.broadcasted_iota(jnp.int32, (2 * SUB, SUB), 0)
    lane_k = lax.broadcasted_iota(jnp.int32, (2 * SUB, SUB), 1)
    causal = lane_k < jnp.where(row < SUB, row, row - SUB)
    meta_visible = lax.broadcasted_iota(jnp.int32, (2 * SUB, LANES), 1) < N_META

    def stacked_q(s):
        q = q_ref[SUB * s:SUB * (s + 1), :].astype(F32) * (HEAD_DIM ** -0.5)
        return jnp.concatenate([jnp.where(first, q, 0.0), jnp.where(first, 0.0, q)],
                               axis=0).astype(BF16)

    def cores(items):
        zs = [_dot_nt(qs, kb) for qs, kb, _, _ in items]
        sps = []
        for z, (_, _, _, visible) in zip(zs, items):
            zb = z.astype(BF16)
            sp = jnp.maximum(zb, 0) + jnp.log(1 + jnp.exp(-jnp.abs(zb)))
            sps.append(sp if visible is None else jnp.where(visible, sp, 0))
        css = [jnp.dot(sp, u[:sp.shape[1], :sp.shape[1]], preferred_element_type=F32)
               for sp in sps]
        out = []
        for z, sp, cs, (_, _, vb, visible) in zip(zs, sps, css, items):
            w = jnp.exp(z - sp.astype(F32) - cs)
            if visible is not None:
                w = jnp.where(visible, w, 0.0)
            wcat = jnp.concatenate([w[:SUB], w[SUB:]], axis=1).astype(BF16)
            first_v = lax.broadcasted_iota(jnp.int32, vb.shape, 1) < HEAD_DIM
            vbd = jnp.concatenate([jnp.where(first_v, vb, 0), jnp.where(first_v, 0, vb)], axis=0)
            out.append((jnp.dot(wcat, vbd, preferred_element_type=F32), cs[:, 0:1] + sp[:, 0:1]))
        return out

    def nearer(carry):
        f = jnp.exp(-carry)
        return jnp.where(first, f[:SUB], f[SUB:])

    def keys(j):
        start = pl.multiple_of(j * SUB, SUB)
        return k_ref[pl.ds(start, SUB), :], v_ref[pl.ds(start, SUB), :]

    qss = [stacked_q(s) for s in range(n_sub)]
    items = []
    for s in range(n_sub):
        ii = i * n_sub + s
        items.append((qss[s],) + keys(ii) + (causal,))
        items.append((qss[s],) + keys(jnp.maximum(ii - 1, 0)) + (None,))
    first_two = cores(items)

    carries, accs = [], []
    for s in range(n_sub):
        (pv_d, tot_d), (pv_p, tot_p) = first_two[2 * s], first_two[2 * s + 1]
        has_prev = (i * n_sub + s > 0).astype(F32)
        accs.append(pv_d + pv_p * (nearer(tot_d) * has_prev))
        carries.append(tot_d + tot_p * has_prev)

    def alive(carry):
        return jnp.min(carry) <= DEAD_CARRY

    def farther(*operands):
        out = []
        for s in range(n_sub):
            carry, acc = operands[s], operands[n_sub + s]

            def body(state, s=s):
                j, _, carry, acc = state
                (pv, tot), = cores([(qss[s],) + keys(j) + (None,)])
                carry_new = carry + tot
                return j - 1, alive(carry_new), carry_new, acc + pv * nearer(carry)

            _, still, carry, acc = lax.while_loop(
                lambda state: (state[0] >= 0) & state[1], body,
                (i * n_sub + s - 2, alive(carry), carry, acc))

            def with_meta(carry, acc, s=s):
                (pv, _), = cores([(qss[s], km_ref[...], vm_ref[...], meta_visible)])
                return acc + pv * nearer(carry)

            out.append(lax.cond(still, with_meta, lambda carry, acc: acc, carry, acc))
        return tuple(out)

    any_alive = functools.reduce(jnp.logical_or, [alive(carry) for carry in carries])
    accs = lax.cond(any_alive, farther, lambda *operands: tuple(operands[n_sub:]), *carries, *accs)
    for s in range(n_sub):
        g = g_ref[SUB * s:SUB * (s + 1), :].astype(F32)
        o_ref[SUB * s:SUB * (s + 1), :] = (accs[s] * (g * _sigmoid(g))).astype(o_ref.dtype)


def _sb_attn(p_main, k_meta, v_meta, u, batch, seq, tq):
    nq = seq // tq

    def seq_map(seg):
        return lambda b, p, i: (b, seg * PAIRS + p)

    def tile_map(seg):
        return lambda b, p, i: (b * nq + i, seg * PAIRS + p)

    return pl.pallas_call(
        functools.partial(_sb_kernel, tq=tq),
        grid=(batch, PAIRS, nq),
        in_specs=[
            pl.BlockSpec((tq, LANES), tile_map(SEG_Q)),
            pl.BlockSpec((seq, LANES), seq_map(SEG_KS)),
            pl.BlockSpec((seq, LANES), seq_map(SEG_VS)),
            pl.BlockSpec((LANES, LANES), lambda b, p, i: (0, p)),
            pl.BlockSpec((LANES, LANES), lambda b, p, i: (0, p)),
            pl.BlockSpec((tq, LANES), tile_map(SEG_GS)),
            pl.BlockSpec((SUB, SUB), lambda b, p, i: (0, 0)),
        ],
        out_specs=pl.BlockSpec((tq, LANES), lambda b, p, i: (b * nq + i, p)),
        out_shape=jax.ShapeDtypeStruct((batch * seq, WIDTH), BF16),
        compiler_params=pltpu.CompilerParams(
            dimension_semantics=("arbitrary", "arbitrary", "arbitrary"),
            vmem_limit_bytes=VMEM_LIMIT),
        name="sb_attn",
    )(p_main, p_main, p_main, k_meta, v_meta, p_main, u)


def _out_kernel(yr_ref, ys_ref, mr_ref, ms_ref, x_ref, wr_ref, ws_ref, wo_ref, g_ref, o_ref):
    pr = jnp.dot(yr_ref[...], wr_ref[...], preferred_element_type=F32)
    ps = jnp.dot(ys_ref[...], ws_ref[...], preferred_element_type=F32)
    mixed = (_sigmoid(mr_ref[...].astype(F32)) * pr + _sigmoid(ms_ref[...].astype(F32)) * ps)
    o = jnp.dot(mixed.astype(BF16), wo_ref[...], preferred_element_type=F32)
    inv = lax.rsqrt(jnp.mean(o * o, axis=-1, keepdims=True) + RMS_EPS)
    o_ref[...] = x_ref[...] + o * inv * g_ref[...]


def _out_proj(y_rwkv, y_sb, p_main, x2d, w_r, w_s, w_o, g, tm):
    rows = x2d.shape[0]
    w_spec = pl.BlockSpec((WIDTH, D_MODEL), lambda i: (0, 0))
    return pl.pallas_call(
        _out_kernel,
        grid=(rows // tm,),
        in_specs=[
            pl.BlockSpec((tm, WIDTH), lambda i: (i, 0)),
            pl.BlockSpec((tm, WIDTH), lambda i: (i, 0)),
            pl.BlockSpec((tm, WIDTH), lambda i: (i, SEG_MR)),
            pl.BlockSpec((tm, WIDTH), lambda i: (i, SEG_MS)),
            pl.BlockSpec((tm, D_MODEL), lambda i: (i, 0)),
            w_spec, w_spec, w_spec,
            pl.BlockSpec((1, D_MODEL), lambda i: (0, 0)),
        ],
        out_specs=pl.BlockSpec((tm, D_MODEL), lambda i: (i, 0)),
        out_shape=jax.ShapeDtypeStruct((rows, D_MODEL), F32),
        compiler_params=pltpu.CompilerParams(
            dimension_semantics=("arbitrary",), vmem_limit_bytes=VMEM_LIMIT),
        name="out_proj",
    )(y_rwkv, y_sb, p_main, p_main, x2d, w_r, w_s, w_o, g)


def _tiles(batch, seq):
    rows = batch * seq
    tm_in = min(1024, rows)
    tm_out = min(512, rows)
    tq = min(2048, seq)
    return tm_in, 2560, tm_out, tq


def kernel(x, meta_tokens, pre_norm_g, post_norm_g, w_in, rwkv_mu, rwkv_w0, rwkv_w_up, rwkv_a0,
           rwkv_a_up, rwkv_k_k, rwkv_k_a, rwkv_r_k, rwkv_gn_g, rwkv_gn_b, w_proj_rwkv, w_proj_sb,
           w_out):
    batch, seq, d_model = x.shape
    assert d_model == D_MODEL and w_in.shape[0] == 1, "single-layer kernel"
    assert seq % 512 == 0 and meta_tokens.shape == (N_META, D_MODEL)
    tm_in, tn_in, tm_out, tq = _tiles(batch, seq)

    w = w_in[0]
    w_main = jnp.concatenate([w[:, :3 * WIDTH], w[:, 3 * WIDTH + 2 * LORA:]], axis=1).astype(BF16)
    w_lo = w[:, 3 * WIDTH:3 * WIDTH + 2 * LORA].astype(BF16)
    g_pre = pre_norm_g[0][None, :]

    x2d = x.reshape(batch * seq, D_MODEL)
    p_main, lora_main = _in_proj(x2d, g_pre, w_main, w_lo, tm_in, tn_in)
    meta_rows = jnp.zeros((RW_TOK, D_MODEL), F32).at[RW_TOK - N_META:].set(meta_tokens.astype(F32))
    p_meta, lora_meta = _in_proj(meta_rows, g_pre, w_main, w_lo, RW_TOK, tn_in)

    mu = rwkv_mu[0]
    mu3 = mu[:3 * WIDTH].reshape(3, WIDTH)
    mu_lora = mu[3 * WIDTH:][None, :]
    as_row = lambda t: t[0].reshape(1, WIDTH)
    rows_vec = (mu3, as_row(rwkv_w0), as_row(rwkv_a0), as_row(rwkv_k_k), as_row(rwkv_k_a),
                as_row(rwkv_r_k), as_row(rwkv_gn_g), as_row(rwkv_gn_b))
    w_lora_up = jnp.concatenate([rwkv_w_up[0], rwkv_a_up[0]], axis=0).astype(BF16)
    idx = jnp.arange(256)
    e256 = (idx[:, None] // HEAD_DIM == idx[None, :] // HEAD_DIM).astype(BF16)
    y_rwkv = _rwkv(p_main, lora_main, p_meta, lora_meta, rows_vec, mu_lora, w_lora_up, e256,
                   batch, seq)

    kidx = jnp.arange(SUB)
    u_tri = (kidx[:, None] > kidx[None, :]).astype(BF16)
    pad = jnp.zeros((LANES - N_META, WIDTH), BF16)
    meta_seg = lambda seg: jnp.concatenate(
        [p_meta[RW_TOK - N_META:, seg * WIDTH:(seg + 1) * WIDTH], pad], axis=0)
    y_sb = _sb_attn(p_main, meta_seg(SEG_KS), meta_seg(SEG_VS), u_tri, batch, seq, tq)

    out = _out_proj(y_rwkv, y_sb, p_main, x2d, w_proj_rwkv[0].astype(BF16),
                    w_proj_sb[0].astype(BF16), w_out[0].astype(BF16), post_norm_g[0][None, :],
                    tm_out)
    return out.reshape(batch, seq, D_MODEL)
```

```python
import functools

import jax
import jax.numpy as jnp
from jax import lax
from jax.experimental import pallas as pl
from jax.experimental.pallas import tpu as pltpu

F32 = jnp.float32
BF16 = jnp.bfloat16

D_MODEL = 1024
N_META = 16
HEADS = 16
HEAD_DIM = 64
WIDTH = HEADS * HEAD_DIM
LORA = 64
RMS_EPS = 1e-6
GN_EPS = 64e-5
L2_EPS = 1e-12
DECAY_SCALE = 0.6065306597126334
LOG2E = 1.4426950408889634

LANES = 128
PAIRS = WIDTH // LANES
CHUNK = 64
RW_CHUNKS = 4
RW_TOK = RW_CHUNKS * CHUNK
SUB = 256
DEAD_CARRY = 110.0
N_SEG = 10
SEG_R, SEG_K, SEG_V, SEG_GR, SEG_Q, SEG_KS, SEG_VS, SEG_GS, SEG_MR, SEG_MS = range(N_SEG)
VMEM_LIMIT = 56 * 1024 * 1024


def _dot(a, b):
    return jnp.dot(a.astype(BF16), b.astype(BF16), preferred_element_type=F32)


def _dot_nt(a, b):
    return lax.dot_general(a.astype(BF16), b.astype(BF16), (((1,), (1,)), ((), ())),
                           preferred_element_type=F32)


def _sigmoid(x):
    return 1.0 / (1.0 + jnp.exp(-x))


def _in_proj_kernel(x_ref, g_ref, w_ref, wl_ref, p_ref, lora_ref, hn_ref):
    @pl.when(pl.program_id(1) == 0)
    def _():
        x = x_ref[...]
        inv = lax.rsqrt(jnp.mean(x * x, axis=-1, keepdims=True) + RMS_EPS)
        hn = (x * inv * g_ref[...]).astype(BF16)
        hn_ref[...] = hn
        lora_ref[...] = jnp.dot(hn, wl_ref[...], preferred_element_type=F32)

    p_ref[...] = jnp.dot(hn_ref[...], w_ref[...], preferred_element_type=F32).astype(p_ref.dtype)


def _in_proj(x2d, g, w_main, w_lora, tm, tn):
    rows = x2d.shape[0]
    cols = w_main.shape[1]
    return pl.pallas_call(
        _in_proj_kernel,
        grid=(rows // tm, cols // tn),
        in_specs=[
            pl.BlockSpec((tm, D_MODEL), lambda i, j: (i, 0)),
            pl.BlockSpec((1, D_MODEL), lambda i, j: (0, 0)),
            pl.BlockSpec((D_MODEL, tn), lambda i, j: (0, j)),
            pl.BlockSpec((D_MODEL, 2 * LORA), lambda i, j: (0, 0)),
        ],
        out_specs=[
            pl.BlockSpec((tm, tn), lambda i, j: (i, j)),
            pl.BlockSpec((tm, 2 * LORA), lambda i, j: (i, 0)),
        ],
        out_shape=[
            jax.ShapeDtypeStruct((rows, cols), BF16),
            jax.ShapeDtypeStruct((rows, 2 * LORA), F32),
        ],
        scratch_shapes=[pltpu.VMEM((tm, D_MODEL), BF16)],
        compiler_params=pltpu.CompilerParams(
            dimension_semantics=("arbitrary", "arbitrary"), vmem_limit_bytes=VMEM_LIMIT),
        name="in_proj",
    )(x2d, g, w_main, w_lora)


def _head_sum(x, e256):
    rows = x.shape[0]
    groups = WIDTH // 256
    stacked = jnp.concatenate([x[:, 256 * g:256 * (g + 1)] for g in range(groups)], axis=0)
    s = _dot(stacked, e256)
    return jnp.concatenate([s[rows * g:rows * (g + 1), :] for g in range(groups)], axis=1)


def _rwkv_kernel(r_ref, k_ref, v_ref, gr_ref, lo_ref, pr_ref, pk_ref, pv_ref, plo_ref,
                 mu_ref, mul_ref, w0_ref, a0_ref, kk_ref, ka_ref, rk_ref, gng_ref, gnb_ref,
                 wl_ref, e256_ref, o_ref, s_ref, last_ref, lastl_ref):
    c = pl.program_id(1)
    is_meta = c == 0

    @pl.when(is_meta)
    def _():
        s_ref[...] = jnp.zeros_like(s_ref)
        last_ref[...] = jnp.zeros_like(last_ref)
        lastl_ref[...] = jnp.zeros_like(lastl_ref)

    def pick(meta_ref, main_ref):
        return jnp.where(is_meta, meta_ref[...].astype(F32), main_ref[...].astype(F32))

    tok = lax.broadcasted_iota(jnp.int32, (RW_TOK, 1), 0)

    def shift(cur, last_row, mu):
        prev = jnp.where(tok == 0, last_row, pltpu.roll(cur, 1, 0))
        return cur + (prev - cur) * mu

    r_raw = pick(pr_ref, r_ref)
    k_raw = pick(pk_ref, k_ref)
    v_raw = pick(pv_ref, v_ref)
    lo_raw = pick(plo_ref, lo_ref)
    r = shift(r_raw, last_ref[0:1, :], mu_ref[0:1, :])
    k = shift(k_raw, last_ref[1:2, :], mu_ref[1:2, :])
    v = shift(v_raw, last_ref[2:3, :], mu_ref[2:3, :])
    lo = shift(lo_raw, lastl_ref[0:1, :], mul_ref[...])
    last_ref[0:1, :] = r_raw[RW_TOK - 1:RW_TOK, :]
    last_ref[1:2, :] = k_raw[RW_TOK - 1:RW_TOK, :]
    last_ref[2:3, :] = v_raw[RW_TOK - 1:RW_TOK, :]
    lastl_ref[0:1, :] = lo_raw[RW_TOK - 1:RW_TOK, :]

    chunk_shift = CHUNK.bit_length() - 1
    ti = lax.broadcasted_iota(jnp.int32, (CHUNK, CHUNK), 0)
    si = lax.broadcasted_iota(jnp.int32, (CHUNK, CHUNK), 1)
    tri = (si <= ti).astype(BF16)
    lane_l = lax.broadcasted_iota(jnp.int32, (CHUNK, 2 * LORA), 1)
    e256 = e256_ref[...]
    w_lora = wl_ref[...]

    row = lax.broadcasted_iota(jnp.int32, (CHUNK, 1), 0)
    lane = lax.broadcasted_iota(jnp.int32, (CHUNK, LANES), 1)
    first = lane < HEAD_DIM
    col = jnp.where(first, lane, lane - HEAD_DIM)
    strict = col < row
    incl = col <= row
    row2 = lax.broadcasted_iota(jnp.int32, (2 * CHUNK, LANES), 0)
    lane2 = lax.broadcasted_iota(jnp.int32, (2 * CHUNK, LANES), 1)
    same_head = (row2 < HEAD_DIM) == (lane2 < HEAD_DIM)
    eye2 = (row2 == lane2).astype(F32)
    pairs = range(PAIRS)

    def blockdiag(t):
        return jnp.concatenate([jnp.where(first, t, 0.0), jnp.where(first, 0.0, t)], axis=0)

    def pair(x, p):
        return x[:, LANES * p:LANES * (p + 1)]

    def prepare(ch, a):
        rows = slice(CHUNK * ch, CHUNK * (ch + 1))
        r_c, k_c, v_c, lo_c = r[rows], k[rows], v[rows], lo[rows]
        pre_w = _dot(jnp.where(lane_l < LORA, jnp.tanh(lo_c), 0.0), w_lora)
        pre_a = _dot(jnp.where(lane_l >= LORA, lo_c, 0.0), w_lora)
        kk = k_c * kk_ref[...]
        kk_sq = _head_sum(kk * kk, e256)
        yield
        logw = -DECAY_SCALE * _sigmoid(w0_ref[...] + pre_w)
        a_lr = _sigmoid(a0_ref[...] + pre_a)
        lw_hi = logw.astype(BF16)
        lw_lo = (logw - lw_hi.astype(F32)).astype(BF16)
        cw = jnp.dot(jnp.concatenate([tri, tri], axis=1), jnp.concatenate([lw_hi, lw_lo], axis=0),
                     preferred_element_type=F32)
        yield
        w_in = jnp.exp(cw)
        w_ex = jnp.exp(cw - logw)
        w_inv = jnp.exp(-cw)
        kappa = kk * lax.rsqrt(kk_sq + L2_EPS)
        k_rep = k_c * (1.0 + (a_lr - 1.0) * ka_ref[...])
        kappa_h = kappa * w_ex
        r_h = r_c * w_in
        k_h = k_rep * w_inv
        b_h = kappa * a_lr * w_inv
        a.update(
            r=r_c, k_rep=k_rep, v=v_c, w_end=w_in[CHUNK - 1:CHUNK, :],
            lhs=[jnp.concatenate([pair(kappa_h, p), pair(r_h, p)], axis=0).astype(BF16)
                 for p in pairs],
            kb=[jnp.concatenate([pair(k_h, p), pair(b_h, p)], axis=0).astype(BF16) for p in pairs],
            kbd=[jnp.concatenate([blockdiag(pair(k_h, p)), blockdiag(pair(b_h, p))],
                                 axis=0).astype(BF16) for p in pairs],
            vbd=[blockdiag(pair(v_c, p)).astype(BF16) for p in pairs])

    def independent(a):
        gram = [_dot_nt(a["lhs"][p], a["kbd"][p]) for p in pairs]
        yield
        l_bd = [blockdiag(jnp.where(strict, g[:CHUNK, LANES:], 0.0)) for g in gram]
        mk = [jnp.concatenate([jnp.where(strict, g[:CHUNK, :LANES], 0.0),
                               jnp.where(incl, g[CHUNK:, :LANES], 0.0)], axis=0) for g in gram]
        a["mb_r"] = [jnp.where(incl, g[CHUNK:, LANES:], 0.0) for g in gram]
        a["mv"] = [_dot(mk[p], a["vbd"][p]) for p in pairs]
        yield
        t_inv = None
        for level in range(chunk_shift):
            size = 1 << level
            rb = jnp.right_shift(row2, level)
            off = ((rb & 1) == 1) & (jnp.right_shift(lane2, level) == rb - 1)
            l_off = [jnp.where(off, l, 0.0) for l in l_bd]
            if level == 0:
                t_inv = [eye2 - l for l in l_off]
            elif size < 8:
                lt = [_dot(l_off[p], t_inv[p]) for p in pairs]
                yield
                t_inv = [t_inv[p] - _dot(t_inv[p], lt[p]) for p in pairs]
                yield
            else:
                odd = [slice(start, start + size) for start in range(size, 2 * CHUNK, 2 * size)]

                def take(x):
                    return jnp.concatenate([x[rows] for rows in odd], axis=0)

                def spread(x_odd, x_even):
                    pieces = []
                    for n, rows in enumerate(odd):
                        pieces += [x_even[rows.start - size:rows.start], x_odd[size * n:size * (n + 1)]]
                    return jnp.concatenate(pieces, axis=0)

                lt_odd = [_dot(take(l_off[p]), t_inv[p]) for p in pairs]
                yield
                t_odd = [take(t_inv[p]) for p in pairs]
                zero = jnp.zeros((2 * CHUNK, LANES), F32)
                delta = [_dot(t_odd[p], spread(lt_odd[p], zero)) for p in pairs]
                yield
                t_inv = [spread(t_odd[p] - delta[p], t_inv[p]) for p in pairs]
        a["t_inv"] = t_inv

    state = [s_ref[p] for p in pairs]

    def dependent(a):
        sp = [_dot_nt(a["lhs"][p], state[p]) for p in pairs]
        yield
        u_bd = [_dot(a["t_inv"][p], blockdiag(sp[p][:CHUNK] + a["mv"][p][:CHUNK])) for p in pairs]
        yield
        vu_t = [jnp.concatenate([pair(a["v"], p), -(u_bd[p][:CHUNK] + u_bd[p][CHUNK:])], axis=0).T
                for p in pairs]
        ds = [_dot(vu_t[p], a["kb"][p]) for p in pairs]
        yd = [_dot(a["mb_r"][p], u_bd[p]) for p in pairs]
        yield
        for p in pairs:
            state[p] = (state[p] + jnp.where(same_head, ds[p], 0.0)) * pair(a["w_end"], p)
        a["y"] = jnp.concatenate(
            [sp[p][CHUNK:] + a["mv"][p][CHUNK:] - yd[p] for p in pairs], axis=1)

    def finish(ch, a):
        y = a["y"]
        mean = _head_sum(y, e256) * (1.0 / HEAD_DIM)
        yield
        d = y - mean
        var = _head_sum(d * d, e256) * (1.0 / HEAD_DIM)
        yield
        y = d * lax.rsqrt(var + GN_EPS) * gng_ref[...] + gnb_ref[...]
        bonus = _head_sum(a["r"] * a["k_rep"] * rk_ref[...], e256) * a["v"]
        yield
        rows = slice(CHUNK * ch, CHUNK * (ch + 1))
        g = jnp.where(is_meta, 0.0, gr_ref[rows, :].astype(F32))
        o_ref[rows, :] = ((y + bonus) * (g * _sigmoid(g))).astype(o_ref.dtype)

    chunks = [dict() for _ in range(RW_CHUNKS)]
    for ch in range(min(2, RW_CHUNKS)):
        for _ in prepare(ch, chunks[ch]):
            pass
    to_prepare = list(range(2, RW_CHUNKS))
    waiting = list(range(RW_CHUNKS))
    prep, running, finished, dep, dep_ch, tails = None, [], set(), None, 0, []
    while waiting or running or dep is not None or dep_ch < RW_CHUNKS or tails:
        if prep is None and to_prepare:
            prep = prepare(to_prepare[0], chunks[to_prepare[0]])
        while waiting and len(running) < 2 and "lhs" in chunks[waiting[0]]:
            ch = waiting.pop(0)
            running.append((ch, independent(chunks[ch])))
        if dep is None and dep_ch in finished:
            dep = dependent(chunks[dep_ch])
        for item in list(running):
            if next(item[1], "done") == "done":
                running.remove(item)
                finished.add(item[0])
        if prep is not None and next(prep, "done") == "done":
            prep = None
            to_prepare.pop(0)
        if dep is not None and next(dep, "done") == "done":
            tails.append(finish(dep_ch, chunks[dep_ch]))
            dep, dep_ch = None, dep_ch + 1
        for gen in list(tails):
            if next(gen, "done") == "done":
                tails.remove(gen)
    for p in pairs:
        s_ref[p] = state[p]


def _rwkv(p_main, lora_main, p_meta, lora_meta, rows_vec, mu_lora, w_lora, e256, batch, seq):
    per_b = seq // RW_TOK

    def main_map(seg):
        return lambda b, c: (b * per_b + jnp.maximum(c - 1, 0), seg)

    def meta_map(seg):
        return lambda b, c: (0, seg)

    row_spec = pl.BlockSpec((1, WIDTH), lambda b, c: (0, 0))
    in_specs = [
        pl.BlockSpec((RW_TOK, WIDTH), main_map(SEG_R)),
        pl.BlockSpec((RW_TOK, WIDTH), main_map(SEG_K)),
        pl.BlockSpec((RW_TOK, WIDTH), main_map(SEG_V)),
        pl.BlockSpec((RW_TOK, WIDTH), main_map(SEG_GR)),
        pl.BlockSpec((RW_TOK, 2 * LORA), main_map(0)),
        pl.BlockSpec((RW_TOK, WIDTH), meta_map(SEG_R)),
        pl.BlockSpec((RW_TOK, WIDTH), meta_map(SEG_K)),
        pl.BlockSpec((RW_TOK, WIDTH), meta_map(SEG_V)),
        pl.BlockSpec((RW_TOK, 2 * LORA), meta_map(0)),
        pl.BlockSpec((3, WIDTH), lambda b, c: (0, 0)),
        pl.BlockSpec((1, 2 * LORA), lambda b, c: (0, 0)),
    ] + [row_spec] * 7 + [
        pl.BlockSpec((2 * LORA, WIDTH), lambda b, c: (0, 0)),
        pl.BlockSpec((256, 256), lambda b, c: (0, 0)),
    ]
    mu3, w0, a0, k_k, k_a, r_k, gn_g, gn_b = rows_vec
    return pl.pallas_call(
        _rwkv_kernel,
        grid=(batch, per_b + 1),
        in_specs=in_specs,
        out_specs=pl.BlockSpec((RW_TOK, WIDTH), main_map(0)),
        out_shape=jax.ShapeDtypeStruct((batch * seq, WIDTH), BF16),
        scratch_shapes=[
            pltpu.VMEM((PAIRS, LANES, LANES), F32),
            pltpu.VMEM((8, WIDTH), F32),
            pltpu.VMEM((8, 2 * LORA), F32),
        ],
        compiler_params=pltpu.CompilerParams(
            dimension_semantics=("arbitrary", "arbitrary"), vmem_limit_bytes=VMEM_LIMIT),
        name="rwkv",
    )(p_main, p_main, p_main, p_main, lora_main, p_meta, p_meta, p_meta, lora_meta,
      mu3, mu_lora, w0, a0, k_k, k_a, r_k, gn_g, gn_b, w_lora, e256)


def _sb_kernel(q_ref, k_ref, v_ref, km_ref, vm_ref, g_ref, u_ref, o_ref, *, tq):
    i = pl.program_id(2)
    n_sub = tq // SUB
    u = u_ref[...]
    first = lax.broadcasted_iota(jnp.int32, (SUB, LANES), 1) < HEAD_DIM
    row = lax.broadcasted_iota(jnp.int32, (2 * SUB, SUB), 0)
    lane_k = lax.broadcasted_iota(jnp.int32, (2 * SUB, SUB), 1)
    causal = lane_k < jnp.where(row < SUB, row, row - SUB)
    meta_visible = lax.broadcasted_iota(jnp.int32, (2 * SUB, LANES), 1) < N_META

    def stacked_q(s):
        q = q_ref[SUB * s:SUB * (s + 1), :].astype(F32) * (HEAD_DIM ** -0.5)
        return jnp.concatenate([jnp.where(first, q, 0.0), jnp.where(first, 0.0, q)],
                               axis=0).astype(BF16)

    def cores(items):
        zs = [_dot_nt(qs, kb) for qs, kb, _, _ in items]
        sps = []
        for z, (_, _, _, visible) in zip(zs, items):
            sp = jnp.maximum(z, 0.0) + jnp.log(1.0 + jnp.exp2(jnp.abs(z) * (-LOG2E)))
            sps.append(sp if visible is None else jnp.where(visible, sp, 0.0))
        css = [jnp.dot(sp.astype(BF16), u[:sp.shape[1], :sp.shape[1]], preferred_element_type=F32)
               for sp in sps]
        out = []
        for z, sp, cs, (_, _, vb, visible) in zip(zs, sps, css, items):
            w = jnp.exp(z - sp - cs)
            if visible is not None:
                w = jnp.where(visible, w, 0.0)
            wcat = jnp.concatenate([w[:SUB], w[SUB:]], axis=1).astype(BF16)
            first_v = lax.broadcasted_iota(jnp.int32, vb.shape, 1) < HEAD_DIM
            vbd = jnp.concatenate([jnp.where(first_v, vb, 0), jnp.where(first_v, 0, vb)], axis=0)
            out.append((jnp.dot(wcat, vbd, preferred_element_type=F32), cs[:, 0:1] + sp[:, 0:1]))
        return out

    def nearer(carry):
        f = jnp.exp(-carry)
        return jnp.where(first, f[:SUB], f[SUB:])

    def keys(j):
        start = pl.multiple_of(j * SUB, SUB)
        return k_ref[pl.ds(start, SUB), :], v_ref[pl.ds(start, SUB), :]

    qss = [stacked_q(s) for s in range(n_sub)]
    items = []
    for s in range(n_sub):
        ii = i * n_sub + s
        items.append((qss[s],) + keys(ii) + (causal,))
        items.append((qss[s],) + keys(jnp.maximum(ii - 1, 0)) + (None,))
    first_two = cores(items)

    carries, accs = [], []
    for s in range(n_sub):
        (pv_d, tot_d), (pv_p, tot_p) = first_two[2 * s], first_two[2 * s + 1]
        has_prev = (i * n_sub + s > 0).astype(F32)
        accs.append(pv_d + pv_p * (nearer(tot_d) * has_prev))
        carries.append(tot_d + tot_p * has_prev)

    def alive(carry):
        return jnp.min(carry) <= DEAD_CARRY

    def farther(*operands):
        out = []
        for s in range(n_sub):
            carry, acc = operands[s], operands[n_sub + s]

            def body(state, s=s):
                j, _, carry, acc = state
                (pv, tot), = cores([(qss[s],) + keys(j) + (None,)])
                carry_new = carry + tot
                return j - 1, alive(carry_new), carry_new, acc + pv * nearer(carry)

            _, still, carry, acc = lax.while_loop(
                lambda state: (state[0] >= 0) & state[1], body,
                (i * n_sub + s - 2, alive(carry), carry, acc))

            def with_meta(carry, acc, s=s):
                (pv, _), = cores([(qss[s], km_ref[...], vm_ref[...], meta_visible)])
                return acc + pv * nearer(carry)

            out.append(lax.cond(still, with_meta, lambda carry, acc: acc, carry, acc))
        return tuple(out)

    any_alive = functools.reduce(jnp.logical_or, [alive(carry) for carry in carries])
    accs = lax.cond(any_alive, farther, lambda *operands: tuple(operands[n_sub:]), *carries, *accs)
    for s in range(n_sub):
        g = g_ref[SUB * s:SUB * (s + 1), :].astype(F32)
        o_ref[SUB * s:SUB * (s + 1), :] = (accs[s] * (g * _sigmoid(g))).astype(o_ref.dtype)


def _sb_attn(p_main, k_meta, v_meta, u, batch, seq, tq):
    nq = seq // tq

    def seq_map(seg):
        return lambda b, p, i: (b, seg * PAIRS + p)

    def tile_map(seg):
        return lambda b, p, i: (b * nq + i, seg * PAIRS + p)

    return pl.pallas_call(
        functools.partial(_sb_kernel, tq=tq),
        grid=(batch, PAIRS, nq),
        in_specs=[
            pl.BlockSpec((tq, LANES), tile_map(SEG_Q)),
            pl.BlockSpec((seq, LANES), seq_map(SEG_KS)),
            pl.BlockSpec((seq, LANES), seq_map(SEG_VS)),
            pl.BlockSpec((LANES, LANES), lambda b, p, i: (0, p)),
            pl.BlockSpec((LANES, LANES), lambda b, p, i: (0, p)),
            pl.BlockSpec((tq, LANES), tile_map(SEG_GS)),
            pl.BlockSpec((SUB, SUB), lambda b, p, i: (0, 0)),
        ],
        out_specs=pl.BlockSpec((tq, LANES), lambda b, p, i: (b * nq + i, p)),
        out_shape=jax.ShapeDtypeStruct((batch * seq, WIDTH), BF16),
        compiler_params=pltpu.CompilerParams(
            dimension_semantics=("arbitrary", "arbitrary", "arbitrary"),
            vmem_limit_bytes=VMEM_LIMIT),
        name="sb_attn",
    )(p_main, p_main, p_main, k_meta, v_meta, p_main, u)


def _out_kernel(yr_ref, ys_ref, mr_ref, ms_ref, x_ref, wr_ref, ws_ref, wo_ref, g_ref, o_ref):
    pr = jnp.dot(yr_ref[...], wr_ref[...], preferred_element_type=F32)
    ps = jnp.dot(ys_ref[...], ws_ref[...], preferred_element_type=F32)
    mixed = (_sigmoid(mr_ref[...].astype(F32)) * pr + _sigmoid(ms_ref[...].astype(F32)) * ps)
    o = jnp.dot(mixed.astype(BF16), wo_ref[...], preferred_element_type=F32)
    inv = lax.rsqrt(jnp.mean(o * o, axis=-1, keepdims=True) + RMS_EPS)
    o_ref[...] = x_ref[...] + o * inv * g_ref[...]


def _out_proj(y_rwkv, y_sb, p_main, x2d, w_r, w_s, w_o, g, tm):
    rows = x2d.shape[0]
    w_spec = pl.BlockSpec((WIDTH, D_MODEL), lambda i: (0, 0))
    return pl.pallas_call(
        _out_kernel,
        grid=(rows // tm,),
        in_specs=[
            pl.BlockSpec((tm, WIDTH), lambda i: (i, 0)),
            pl.BlockSpec((tm, WIDTH), lambda i: (i, 0)),
            pl.BlockSpec((tm, WIDTH), lambda i: (i, SEG_MR)),
            pl.BlockSpec((tm, WIDTH), lambda i: (i, SEG_MS)),
            pl.BlockSpec((tm, D_MODEL), lambda i: (i, 0)),
            w_spec, w_spec, w_spec,
            pl.BlockSpec((1, D_MODEL), lambda i: (0, 0)),
        ],
        out_specs=pl.BlockSpec((tm, D_MODEL), lambda i: (i, 0)),
        out_shape=jax.ShapeDtypeStruct((rows, D_MODEL), F32),
        compiler_params=pltpu.CompilerParams(
            dimension_semantics=("arbitrary",), vmem_limit_bytes=VMEM_LIMIT),
        name="out_proj",
    )(y_rwkv, y_sb, p_main, p_main, x2d, w_r, w_s, w_o, g)


def _tiles(batch, seq):
    rows = batch * seq
    tm_in = min(1024, rows)
    tm_out = min(512, rows)
    tq = min(2048, seq)
    return tm_in, 2560, tm_out, tq


def kernel(x, meta_tokens, pre_norm_g, post_norm_g, w_in, rwkv_mu, rwkv_w0, rwkv_w_up, rwkv_a0,
           rwkv_a_up, rwkv_k_k, rwkv_k_a, rwkv_r_k, rwkv_gn_g, rwkv_gn_b, w_proj_rwkv, w_proj_sb,
           w_out):
    batch, seq, d_model = x.shape
    assert d_model == D_MODEL and w_in.shape[0] == 1, "single-layer kernel"
    assert seq % 512 == 0 and meta_tokens.shape == (N_META, D_MODEL)
    tm_in, tn_in, tm_out, tq = _tiles(batch, seq)

    w = w_in[0]
    w_main = jnp.concatenate([w[:, :3 * WIDTH], w[:, 3 * WIDTH + 2 * LORA:]], axis=1).astype(BF16)
    w_lo = w[:, 3 * WIDTH:3 * WIDTH + 2 * LORA].astype(BF16)
    g_pre = pre_norm_g[0][None, :]

    x2d = x.reshape(batch * seq, D_MODEL)
    p_main, lora_main = _in_proj(x2d, g_pre, w_main, w_lo, tm_in, tn_in)
    meta_rows = jnp.zeros((RW_TOK, D_MODEL), F32).at[RW_TOK - N_META:].set(meta_tokens.astype(F32))
    p_meta, lora_meta = _in_proj(meta_rows, g_pre, w_main, w_lo, RW_TOK, tn_in)

    mu = rwkv_mu[0]
    mu3 = mu[:3 * WIDTH].reshape(3, WIDTH)
    mu_lora = mu[3 * WIDTH:][None, :]
    as_row = lambda t: t[0].reshape(1, WIDTH)
    rows_vec = (mu3, as_row(rwkv_w0), as_row(rwkv_a0), as_row(rwkv_k_k), as_row(rwkv_k_a),
                as_row(rwkv_r_k), as_row(rwkv_gn_g), as_row(rwkv_gn_b))
    w_lora_up = jnp.concatenate([rwkv_w_up[0], rwkv_a_up[0]], axis=0).astype(BF16)
    idx = jnp.arange(256)
    e256 = (idx[:, None] // HEAD_DIM == idx[None, :] // HEAD_DIM).astype(BF16)
    y_rwkv = _rwkv(p_main, lora_main, p_meta, lora_meta, rows_vec, mu_lora, w_lora_up, e256,
                   batch, seq)

    kidx = jnp.arange(SUB)
    u_tri = (kidx[:, None] > kidx[None, :]).astype(BF16)
    pad = jnp.zeros((LANES - N_META, WIDTH), BF16)
    meta_seg = lambda seg: jnp.concatenate(
        [p_meta[RW_TOK - N_META:, seg * WIDTH:(seg + 1) * WIDTH], pad], axis=0)
    y_sb = _sb_attn(p_main, meta_seg(SEG_KS), meta_seg(SEG_VS), u_tri, batch, seq, tq)

    out = _out_proj(y_rwkv, y_sb, p_main, x2d, w_proj_rwkv[0].astype(BF16),
                    w_proj_sb[0].astype(BF16), w_out[0].astype(BF16), post_norm_g[0][None, :],
                    tm_out)
    return out.reshape(batch, seq, D_MODEL)
```

```python
import functools

import jax
import jax.numpy as jnp
from jax import lax
from jax.experimental import pallas as pl
from jax.experimental.pallas import tpu as pltpu

F32 = jnp.float32
BF16 = jnp.bfloat16

D_MODEL = 1024
N_META = 16
HEADS = 16
HEAD_DIM = 64
WIDTH = HEADS * HEAD_DIM
LORA = 64
RMS_EPS = 1e-6
GN_EPS = 64e-5
L2_EPS = 1e-12
DECAY_SCALE = 0.6065306597126334
LOG2E = 1.4426950408889634

LANES = 128
PAIRS = WIDTH // LANES
CHUNK = 64
RW_CHUNKS = 4
RW_TOK = RW_CHUNKS * CHUNK
SUB = 256
DEAD_CARRY = 110.0
N_SEG = 10
SEG_R, SEG_K, SEG_V, SEG_GR, SEG_Q, SEG_KS, SEG_VS, SEG_GS, SEG_MR, SEG_MS = range(N_SEG)
VMEM_LIMIT = 56 * 1024 * 1024


def _dot(a, b):
    return jnp.dot(a.astype(BF16), b.astype(BF16), preferred_element_type=F32)


def _dot_nt(a, b):
    return lax.dot_general(a.astype(BF16), b.astype(BF16), (((1,), (1,)), ((), ())),
                           preferred_element_type=F32)


def _sigmoid(x):
    return 1.0 / (1.0 + jnp.exp(-x))


def _in_proj_kernel(x_ref, g_ref, w_ref, wl_ref, p_ref, lora_ref, hn_ref):
    @pl.when(pl.program_id(1) == 0)
    def _():
        x = x_ref[...]
        inv = lax.rsqrt(jnp.mean(x * x, axis=-1, keepdims=True) + RMS_EPS)
        hn = (x * inv * g_ref[...]).astype(BF16)
        hn_ref[...] = hn
        lora_ref[...] = jnp.dot(hn, wl_ref[...], preferred_element_type=F32)

    p_ref[...] = jnp.dot(hn_ref[...], w_ref[...], preferred_element_type=F32).astype(p_ref.dtype)


def _in_proj(x2d, g, w_main, w_lora, tm, tn):
    rows = x2d.shape[0]
    cols = w_main.shape[1]
    return pl.pallas_call(
        _in_proj_kernel,
        grid=(rows // tm, cols // tn),
        in_specs=[
            pl.BlockSpec((tm, D_MODEL), lambda i, j: (i, 0)),
            pl.BlockSpec((1, D_MODEL), lambda i, j: (0, 0)),
            pl.BlockSpec((D_MODEL, tn), lambda i, j: (0, j)),
            pl.BlockSpec((D_MODEL, 2 * LORA), lambda i, j: (0, 0)),
        ],
        out_specs=[
            pl.BlockSpec((tm, tn), lambda i, j: (i, j)),
            pl.BlockSpec((tm, 2 * LORA), lambda i, j: (i, 0)),
        ],
        out_shape=[
            jax.ShapeDtypeStruct((rows, cols), BF16),
            jax.ShapeDtypeStruct((rows, 2 * LORA), F32),
        ],
        scratch_shapes=[pltpu.VMEM((tm, D_MODEL), BF16)],
        compiler_params=pltpu.CompilerParams(
            dimension_semantics=("arbitrary", "arbitrary"), vmem_limit_bytes=VMEM_LIMIT),
        name="in_proj",
    )(x2d, g, w_main, w_lora)


def _head_sum(x, e256):
    rows = x.shape[0]
    groups = WIDTH // 256
    stacked = jnp.concatenate([x[:, 256 * g:256 * (g + 1)] for g in range(groups)], axis=0)
    s = _dot(stacked, e256)
    return jnp.concatenate([s[rows * g:rows * (g + 1), :] for g in range(groups)], axis=1)


def _rwkv_kernel(r_ref, k_ref, v_ref, gr_ref, lo_ref, s0_ref, last0_ref, lastl0_ref,
                 mu_ref, mul_ref, w0_ref, a0_ref, kk_ref, ka_ref, rk_ref, gng_ref, gnb_ref,
                 wl_ref, e256_ref, o_ref, s_out, last_out, lastl_out, s_ref, last_ref, lastl_ref):
    @pl.when(pl.program_id(1) == 0)
    def _():
        s_ref[...] = s0_ref[...]
        last_ref[...] = last0_ref[...]
        lastl_ref[...] = lastl0_ref[...]

    tok = lax.broadcasted_iota(jnp.int32, (RW_TOK, 1), 0)

    def shift(cur, last_row, mu):
        prev = jnp.where(tok == 0, last_row, pltpu.roll(cur, 1, 0))
        return cur + (prev - cur) * mu

    r_raw = r_ref[...].astype(F32)
    k_raw = k_ref[...].astype(F32)
    v_raw = v_ref[...].astype(F32)
    lo_raw = lo_ref[...]
    r = shift(r_raw, last_ref[0:1, :], mu_ref[0:1, :])
    k = shift(k_raw, last_ref[1:2, :], mu_ref[1:2, :])
    v = shift(v_raw, last_ref[2:3, :], mu_ref[2:3, :])
    lo = shift(lo_raw, lastl_ref[0:1, :], mul_ref[...])
    last_ref[0:1, :] = r_raw[RW_TOK - 1:RW_TOK, :]
    last_ref[1:2, :] = k_raw[RW_TOK - 1:RW_TOK, :]
    last_ref[2:3, :] = v_raw[RW_TOK - 1:RW_TOK, :]
    lastl_ref[0:1, :] = lo_raw[RW_TOK - 1:RW_TOK, :]

    chunk_shift = CHUNK.bit_length() - 1
    ti = lax.broadcasted_iota(jnp.int32, (CHUNK, CHUNK), 0)
    si = lax.broadcasted_iota(jnp.int32, (CHUNK, CHUNK), 1)
    tri = (si <= ti).astype(BF16)
    lane_l = lax.broadcasted_iota(jnp.int32, (CHUNK, 2 * LORA), 1)
    e256 = e256_ref[...]
    w_lora = wl_ref[...]

    row = lax.broadcasted_iota(jnp.int32, (CHUNK, 1), 0)
    lane = lax.broadcasted_iota(jnp.int32, (CHUNK, LANES), 1)
    first = lane < HEAD_DIM
    col = jnp.where(first, lane, lane - HEAD_DIM)
    strict = col < row
    incl = col <= row
    row2 = lax.broadcasted_iota(jnp.int32, (2 * CHUNK, LANES), 0)
    lane2 = lax.broadcasted_iota(jnp.int32, (2 * CHUNK, LANES), 1)
    same_head = (row2 < HEAD_DIM) == (lane2 < HEAD_DIM)
    eye2 = (row2 == lane2).astype(F32)
    pairs = range(PAIRS)

    def blockdiag(t):
        return jnp.concatenate([jnp.where(first, t, 0.0), jnp.where(first, 0.0, t)], axis=0)

    def pair(x, p):
        return x[:, LANES * p:LANES * (p + 1)]

    def prepare(ch, a):
        rows = slice(CHUNK * ch, CHUNK * (ch + 1))
        r_c, k_c, v_c, lo_c = r[rows], k[rows], v[rows], lo[rows]
        pre_w = _dot(jnp.where(lane_l < LORA, jnp.tanh(lo_c), 0.0), w_lora)
        pre_a = _dot(jnp.where(lane_l >= LORA, lo_c, 0.0), w_lora)
        kk = k_c * kk_ref[...]
        kk_sq = _head_sum(kk * kk, e256)
        yield
        logw = -DECAY_SCALE * _sigmoid(w0_ref[...] + pre_w)
        a_lr = _sigmoid(a0_ref[...] + pre_a)
        lw_hi = logw.astype(BF16)
        lw_lo = (logw - lw_hi.astype(F32)).astype(BF16)
        cw = jnp.dot(jnp.concatenate([tri, tri], axis=1), jnp.concatenate([lw_hi, lw_lo], axis=0),
                     preferred_element_type=F32)
        yield
        w_in = jnp.exp(cw)
        w_ex = jnp.exp(cw - logw)
        w_inv = jnp.exp(-cw)
        kappa = kk * lax.rsqrt(kk_sq + L2_EPS)
        k_rep = k_c * (1.0 + (a_lr - 1.0) * ka_ref[...])
        kappa_h = kappa * w_ex
        r_h = r_c * w_in
        k_h = k_rep * w_inv
        b_h = kappa * a_lr * w_inv
        a.update(
            r=r_c, k_rep=k_rep, v=v_c, w_end=w_in[CHUNK - 1:CHUNK, :],
            lhs=[jnp.concatenate([pair(kappa_h, p), pair(r_h, p)], axis=0).astype(BF16)
                 for p in pairs],
            kb=[jnp.concatenate([pair(k_h, p), pair(b_h, p)], axis=0).astype(BF16) for p in pairs],
            kbd=[jnp.concatenate([blockdiag(pair(k_h, p)), blockdiag(pair(b_h, p))],
                                 axis=0).astype(BF16) for p in pairs],
            vbd=[blockdiag(pair(v_c, p)).astype(BF16) for p in pairs])

    def independent(a):
        gram = [_dot_nt(a["lhs"][p], a["kbd"][p]) for p in pairs]
        yield
        l_bd = [blockdiag(jnp.where(strict, g[:CHUNK, LANES:], 0.0)) for g in gram]
        mk = [jnp.concatenate([jnp.where(strict, g[:CHUNK, :LANES], 0.0),
                               jnp.where(incl, g[CHUNK:, :LANES], 0.0)], axis=0) for g in gram]
        a["mb_r"] = [jnp.where(incl, g[CHUNK:, LANES:], 0.0) for g in gram]
        a["mv"] = [_dot(mk[p], a["vbd"][p]) for p in pairs]
        yield
        t_inv = None
        for level in range(chunk_shift):
            size = 1 << level
            rb = jnp.right_shift(row2, level)
            off = ((rb & 1) == 1) & (jnp.right_shift(lane2, level) == rb - 1)
            l_off = [jnp.where(off, l, 0.0) for l in l_bd]
            if level == 0:
                t_inv = [eye2 - l for l in l_off]
            elif size < 8:
                lt = [_dot(l_off[p], t_inv[p]) for p in pairs]
                yield
                t_inv = [t_inv[p] - _dot(t_inv[p], lt[p]) for p in pairs]
                yield
            else:
                odd = [slice(start, start + size) for start in range(size, 2 * CHUNK, 2 * size)]

                def take(x):
                    return jnp.concatenate([x[rows] for rows in odd], axis=0)

                def spread(x_odd, x_even):
                    pieces = []
                    for n, rows in enumerate(odd):
                        pieces += [x_even[rows.start - size:rows.start], x_odd[size * n:size * (n + 1)]]
                    return jnp.concatenate(pieces, axis=0)

                lt_odd = [_dot(take(l_off[p]), t_inv[p]) for p in pairs]
                yield
                t_odd = [take(t_inv[p]) for p in pairs]
                zero = jnp.zeros((2 * CHUNK, LANES), F32)
                delta = [_dot(t_odd[p], spread(lt_odd[p], zero)) for p in pairs]
                yield
                t_inv = [spread(t_odd[p] - delta[p], t_inv[p]) for p in pairs]
        a["t_inv"] = t_inv

    state = [s_ref[p] for p in pairs]

    def dependent(a):
        sp = [_dot_nt(a["lhs"][p], state[p]) for p in pairs]
        yield
        u_bd = [_dot(a["t_inv"][p], blockdiag(sp[p][:CHUNK] + a["mv"][p][:CHUNK])) for p in pairs]
        yield
        vu_t = [jnp.concatenate([pair(a["v"], p), -(u_bd[p][:CHUNK] + u_bd[p][CHUNK:])], axis=0).T
                for p in pairs]
        ds = [_dot(vu_t[p], a["kb"][p]) for p in pairs]
        yd = [_dot(a["mb_r"][p], u_bd[p]) for p in pairs]
        yield
        for p in pairs:
            state[p] = (state[p] + jnp.where(same_head, ds[p], 0.0)) * pair(a["w_end"], p)
        a["y"] = jnp.concatenate(
            [sp[p][CHUNK:] + a["mv"][p][CHUNK:] - yd[p] for p in pairs], axis=1)

    def finish(ch, a):
        y = a["y"]
        mean = _head_sum(y, e256) * (1.0 / HEAD_DIM)
        yield
        d = y - mean
        var = _head_sum(d * d, e256) * (1.0 / HEAD_DIM)
        yield
        y = d * lax.rsqrt(var + GN_EPS) * gng_ref[...] + gnb_ref[...]
        bonus = _head_sum(a["r"] * a["k_rep"] * rk_ref[...], e256) * a["v"]
        yield
        rows = slice(CHUNK * ch, CHUNK * (ch + 1))
        g = gr_ref[rows, :].astype(F32)
        o_ref[rows, :] = ((y + bonus) * (g * _sigmoid(g))).astype(o_ref.dtype)

    chunks = [dict() for _ in range(RW_CHUNKS)]
    for ch in range(min(2, RW_CHUNKS)):
        for _ in prepare(ch, chunks[ch]):
            pass
    to_prepare = list(range(2, RW_CHUNKS))
    waiting = list(range(RW_CHUNKS))
    prep, running, finished, dep, dep_ch, tails = None, [], set(), None, 0, []
    while waiting or running or dep is not None or dep_ch < RW_CHUNKS or tails:
        if prep is None and to_prepare:
            prep = prepare(to_prepare[0], chunks[to_prepare[0]])
        while waiting and len(running) < 2 and "lhs" in chunks[waiting[0]]:
            ch = waiting.pop(0)
            running.append((ch, independent(chunks[ch])))
        if dep is None and dep_ch in finished:
            dep = dependent(chunks[dep_ch])
        for item in list(running):
            if next(item[1], "done") == "done":
                running.remove(item)
                finished.add(item[0])
        if prep is not None and next(prep, "done") == "done":
            prep = None
            to_prepare.pop(0)
        if dep is not None and next(dep, "done") == "done":
            tails.append(finish(dep_ch, chunks[dep_ch]))
            dep, dep_ch = None, dep_ch + 1
        for gen in list(tails):
            if next(gen, "done") == "done":
                tails.remove(gen)
    for p in pairs:
        s_ref[p] = state[p]
        s_out[p] = state[p]
    last_out[...] = last_ref[...]
    lastl_out[...] = lastl_ref[...]


def _rwkv(p, lora, start, rows_vec, mu_lora, w_lora, e256, batch, seq):
    per_b = seq // RW_TOK

    def block(seg):
        return lambda b, c: (b * per_b + c, seg)

    def whole(shape):
        return pl.BlockSpec(shape, lambda b, c: (0,) * len(shape))

    state_shapes = [(PAIRS, LANES, LANES), (8, WIDTH), (8, 2 * LORA)]
    in_specs = [
        pl.BlockSpec((RW_TOK, WIDTH), block(SEG_R)),
        pl.BlockSpec((RW_TOK, WIDTH), block(SEG_K)),
        pl.BlockSpec((RW_TOK, WIDTH), block(SEG_V)),
        pl.BlockSpec((RW_TOK, WIDTH), block(SEG_GR)),
        pl.BlockSpec((RW_TOK, 2 * LORA), block(0)),
    ] + [whole(s) for s in state_shapes] + [whole((3, WIDTH)), whole((1, 2 * LORA))
    ] + [whole((1, WIDTH))] * 7 + [whole((2 * LORA, WIDTH)), whole((256, 256))]
    mu3, w0, a0, k_k, k_a, r_k, gn_g, gn_b = rows_vec
    return pl.pallas_call(
        _rwkv_kernel,
        grid=(batch, per_b),
        in_specs=in_specs,
        out_specs=[pl.BlockSpec((RW_TOK, WIDTH), block(0))] + [whole(s) for s in state_shapes],
        out_shape=[jax.ShapeDtypeStruct((batch * seq, WIDTH), BF16)]
        + [jax.ShapeDtypeStruct(s, F32) for s in state_shapes],
        scratch_shapes=[pltpu.VMEM(s, F32) for s in state_shapes],
        compiler_params=pltpu.CompilerParams(
            dimension_semantics=("arbitrary", "arbitrary"), vmem_limit_bytes=VMEM_LIMIT),
        name="rwkv",
    )(p, p, p, p, lora, *start, mu3, mu_lora, w0, a0, k_k, k_a, r_k, gn_g, gn_b, w_lora, e256)


def _sb_kernel(q_ref, k_ref, v_ref, km_ref, vm_ref, g_ref, u_ref, o_ref, *, tq):
    i = pl.program_id(2)
    n_sub = tq // SUB
    u = u_ref[...]
    first = lax.broadcasted_iota(jnp.int32, (SUB, LANES), 1) < HEAD_DIM
    row = lax.broadcasted_iota(jnp.int32, (2 * SUB, SUB), 0)
    lane_k = lax.broadcasted_iota(jnp.int32, (2 * SUB, SUB), 1)
    causal = lane_k < jnp.where(row < SUB, row, row - SUB)
    meta_visible = lax.broadcasted_iota(jnp.int32, (2 * SUB, LANES), 1) < N_META

    def stacked_q(s):
        q = q_ref[SUB * s:SUB * (s + 1), :].astype(F32) * (HEAD_DIM ** -0.5)
        return jnp.concatenate([jnp.where(first, q, 0.0), jnp.where(first, 0.0, q)],
                               axis=0).astype(BF16)

    def cores(items):
        zs = [_dot_nt(qs, kb) for qs, kb, _, _ in items]
        sps = []
        for z, (_, _, _, visible) in zip(zs, items):
            sp = jnp.maximum(z, 0.0) + jnp.log(1.0 + jnp.exp2(jnp.abs(z) * (-LOG2E)))
            sps.append(sp if visible is None else jnp.where(visible, sp, 0.0))
        css = [jnp.dot(sp.astype(BF16), u[:sp.shape[1], :sp.shape[1]], preferred_element_type=F32)
               for sp in sps]
        out = []
        for z, sp, cs, (_, _, vb, visible) in zip(zs, sps, css, items):
            w = jnp.exp(z - sp - cs)
            if visible is not None:
                w = jnp.where(visible, w, 0.0)
            wcat = jnp.concatenate([w[:SUB], w[SUB:]], axis=1).astype(BF16)
            first_v = lax.broadcasted_iota(jnp.int32, vb.shape, 1) < HEAD_DIM
            vbd = jnp.concatenate([jnp.where(first_v, vb, 0), jnp.where(first_v, 0, vb)], axis=0)
            out.append((jnp.dot(wcat, vbd, preferred_element_type=F32), cs[:, 0:1] + sp[:, 0:1]))
        return out

    def nearer(carry):
        f = jnp.exp(-carry)
        return jnp.where(first, f[:SUB], f[SUB:])

    def keys(j):
        start = pl.multiple_of(j * SUB, SUB)
        return k_ref[pl.ds(start, SUB), :], v_ref[pl.ds(start, SUB), :]

    qss = [stacked_q(s) for s in range(n_sub)]
    items = []
    for s in range(n_sub):
        ii = i * n_sub + s
        items.append((qss[s],) + keys(ii) + (causal,))
        items.append((qss[s],) + keys(jnp.maximum(ii - 1, 0)) + (None,))
    first_two = cores(items)

    carries, accs = [], []
    for s in range(n_sub):
        (pv_d, tot_d), (pv_p, tot_p) = first_two[2 * s], first_two[2 * s + 1]
        has_prev = (i * n_sub + s > 0).astype(F32)
        accs.append(pv_d + pv_p * (nearer(tot_d) * has_prev))
        carries.append(tot_d + tot_p * has_prev)

    def alive(carry):
        return jnp.min(carry) <= DEAD_CARRY

    def farther(*operands):
        out = []
        for s in range(n_sub):
            carry, acc = operands[s], operands[n_sub + s]

            def body(state, s=s):
                j, _, carry, acc = state
                (pv, tot), = cores([(qss[s],) + keys(j) + (None,)])
                carry_new = carry + tot
                return j - 1, alive(carry_new), carry_new, acc + pv * nearer(carry)

            _, still, carry, acc = lax.while_loop(
                lambda state: (state[0] >= 0) & state[1], body,
                (i * n_sub + s - 2, alive(carry), carry, acc))

            def with_meta(carry, acc, s=s):
                (pv, _), = cores([(qss[s], km_ref[...], vm_ref[...], meta_visible)])
                return acc + pv * nearer(carry)

            out.append(lax.cond(still, with_meta, lambda carry, acc: acc, carry, acc))
        return tuple(out)

    any_alive = functools.reduce(jnp.logical_or, [alive(carry) for carry in carries])
    accs = lax.cond(any_alive, farther, lambda *operands: tuple(operands[n_sub:]), *carries, *accs)
    for s in range(n_sub):
        g = g_ref[SUB * s:SUB * (s + 1), :].astype(F32)
        o_ref[SUB * s:SUB * (s + 1), :] = (accs[s] * (g * _sigmoid(g))).astype(o_ref.dtype)


def _sb_attn(p_main, k_meta, v_meta, u, batch, seq, tq):
    nq = seq // tq

    def seq_map(seg):
        return lambda b, p, i: (b, seg * PAIRS + p)

    def tile_map(seg):
        return lambda b, p, i: (b * nq + i, seg * PAIRS + p)

    return pl.pallas_call(
        functools.partial(_sb_kernel, tq=tq),
        grid=(batch, PAIRS, nq),
        in_specs=[
            pl.BlockSpec((tq, LANES), tile_map(SEG_Q)),
            pl.BlockSpec((seq, LANES), seq_map(SEG_KS)),
            pl.BlockSpec((seq, LANES), seq_map(SEG_VS)),
            pl.BlockSpec((LANES, LANES), lambda b, p, i: (0, p)),
            pl.BlockSpec((LANES, LANES), lambda b, p, i: (0, p)),
            pl.BlockSpec((tq, LANES), tile_map(SEG_GS)),
            pl.BlockSpec((SUB, SUB), lambda b, p, i: (0, 0)),
        ],
        out_specs=pl.BlockSpec((tq, LANES), lambda b, p, i: (b * nq + i, p)),
        out_shape=jax.ShapeDtypeStruct((batch * seq, WIDTH), BF16),
        compiler_params=pltpu.CompilerParams(
            dimension_semantics=("arbitrary", "arbitrary", "arbitrary"),
            vmem_limit_bytes=VMEM_LIMIT),
        name="sb_attn",
    )(p_main, p_main, p_main, k_meta, v_meta, p_main, u)


def _out_kernel(yr_ref, ys_ref, mr_ref, ms_ref, x_ref, wr_ref, ws_ref, wo_ref, g_ref, o_ref):
    pr = jnp.dot(yr_ref[...], wr_ref[...], preferred_element_type=F32)
    ps = jnp.dot(ys_ref[...], ws_ref[...], preferred_element_type=F32)
    mixed = (_sigmoid(mr_ref[...].astype(F32)) * pr + _sigmoid(ms_ref[...].astype(F32)) * ps)
    o = jnp.dot(mixed.astype(BF16), wo_ref[...], preferred_element_type=F32)
    inv = lax.rsqrt(jnp.mean(o * o, axis=-1, keepdims=True) + RMS_EPS)
    o_ref[...] = x_ref[...] + o * inv * g_ref[...]


def _out_proj(y_rwkv, y_sb, p_main, x2d, w_r, w_s, w_o, g, tm):
    rows = x2d.shape[0]
    w_spec = pl.BlockSpec((WIDTH, D_MODEL), lambda i: (0, 0))
    return pl.pallas_call(
        _out_kernel,
        grid=(rows // tm,),
        in_specs=[
            pl.BlockSpec((tm, WIDTH), lambda i: (i, 0)),
            pl.BlockSpec((tm, WIDTH), lambda i: (i, 0)),
            pl.BlockSpec((tm, WIDTH), lambda i: (i, SEG_MR)),
            pl.BlockSpec((tm, WIDTH), lambda i: (i, SEG_MS)),
            pl.BlockSpec((tm, D_MODEL), lambda i: (i, 0)),
            w_spec, w_spec, w_spec,
            pl.BlockSpec((1, D_MODEL), lambda i: (0, 0)),
        ],
        out_specs=pl.BlockSpec((tm, D_MODEL), lambda i: (i, 0)),
        out_shape=jax.ShapeDtypeStruct((rows, D_MODEL), F32),
        compiler_params=pltpu.CompilerParams(
            dimension_semantics=("arbitrary",), vmem_limit_bytes=VMEM_LIMIT),
        name="out_proj",
    )(y_rwkv, y_sb, p_main, p_main, x2d, w_r, w_s, w_o, g)


def _tiles(batch, seq):
    rows = batch * seq
    tm_in = min(1024, rows)
    tm_out = min(512, rows)
    tq = min(2048, seq)
    return tm_in, 2560, tm_out, tq


def kernel(x, meta_tokens, pre_norm_g, post_norm_g, w_in, rwkv_mu, rwkv_w0, rwkv_w_up, rwkv_a0,
           rwkv_a_up, rwkv_k_k, rwkv_k_a, rwkv_r_k, rwkv_gn_g, rwkv_gn_b, w_proj_rwkv, w_proj_sb,
           w_out):
    batch, seq, d_model = x.shape
    assert d_model == D_MODEL and w_in.shape[0] == 1, "single-layer kernel"
    assert seq % 512 == 0 and meta_tokens.shape == (N_META, D_MODEL)
    tm_in, tn_in, tm_out, tq = _tiles(batch, seq)

    w = w_in[0]
    w_main = jnp.concatenate([w[:, :3 * WIDTH], w[:, 3 * WIDTH + 2 * LORA:]], axis=1).astype(BF16)
    w_lo = w[:, 3 * WIDTH:3 * WIDTH + 2 * LORA].astype(BF16)
    g_pre = pre_norm_g[0][None, :]

    x2d = x.reshape(batch * seq, D_MODEL)
    p_main, lora_main = _in_proj(x2d, g_pre, w_main, w_lo, tm_in, tn_in)
    meta_rows = jnp.zeros((RW_TOK, D_MODEL), F32).at[RW_TOK - N_META:].set(meta_tokens.astype(F32))
    p_meta, lora_meta = _in_proj(meta_rows, g_pre, w_main, w_lo, RW_TOK, tn_in)

    mu = rwkv_mu[0]
    mu3 = mu[:3 * WIDTH].reshape(3, WIDTH)
    mu_lora = mu[3 * WIDTH:][None, :]
    as_row = lambda t: t[0].reshape(1, WIDTH)
    rows_vec = (mu3, as_row(rwkv_w0), as_row(rwkv_a0), as_row(rwkv_k_k), as_row(rwkv_k_a),
                as_row(rwkv_r_k), as_row(rwkv_gn_g), as_row(rwkv_gn_b))
    w_lora_up = jnp.concatenate([rwkv_w_up[0], rwkv_a_up[0]], axis=0).astype(BF16)
    idx = jnp.arange(256)
    e256 = (idx[:, None] // HEAD_DIM == idx[None, :] // HEAD_DIM).astype(BF16)
    zero_start = (jnp.zeros((PAIRS, LANES, LANES), F32), jnp.zeros((8, WIDTH), F32),
                  jnp.zeros((8, 2 * LORA), F32))
    _, *after_meta = _rwkv(p_meta, lora_meta, zero_start, rows_vec, mu_lora, w_lora_up, e256,
                           1, RW_TOK)
    y_rwkv, *_ = _rwkv(p_main, lora_main, after_meta, rows_vec, mu_lora, w_lora_up, e256,
                       batch, seq)

    kidx = jnp.arange(SUB)
    u_tri = (kidx[:, None] > kidx[None, :]).astype(BF16)
    pad = jnp.zeros((LANES - N_META, WIDTH), BF16)
    meta_seg = lambda seg: jnp.concatenate(
        [p_meta[RW_TOK - N_META:, seg * WIDTH:(seg + 1) * WIDTH], pad], axis=0)
    y_sb = _sb_attn(p_main, meta_seg(SEG_KS), meta_seg(SEG_VS), u_tri, batch, seq, tq)

    out = _out_proj(y_rwkv, y_sb, p_main, x2d, w_proj_rwkv[0].astype(BF16),
                    w_proj_sb[0].astype(BF16), w_out[0].astype(BF16), post_norm_g[0][None, :],
                    tm_out)
    return out.reshape(batch, seq, D_MODEL)
```

```python
import functools

import jax
import jax.numpy as jnp
from jax import lax
from jax.experimental import pallas as pl
from jax.experimental.pallas import tpu as pltpu

F32 = jnp.float32
BF16 = jnp.bfloat16

D_MODEL = 1024
N_META = 16
HEADS = 16
HEAD_DIM = 64
WIDTH = HEADS * HEAD_DIM
LORA = 64
RMS_EPS = 1e-6
GN_EPS = 64e-5
L2_EPS = 1e-12
DECAY_SCALE = 0.6065306597126334
LOG2E = 1.4426950408889634

LANES = 128
PAIRS = WIDTH // LANES
CHUNK = 64
RW_CHUNKS = 4
SUB = 256
DEAD_CARRY = 110.0
N_SEG = 10
SEG_R, SEG_K, SEG_V, SEG_GR, SEG_Q, SEG_KS, SEG_VS, SEG_GS, SEG_MR, SEG_MS = range(N_SEG)
VMEM_LIMIT = 56 * 1024 * 1024


def _dot(a, b):
    return jnp.dot(a.astype(BF16), b.astype(BF16), preferred_element_type=F32)


def _dot_nt(a, b):
    return lax.dot_general(a.astype(BF16), b.astype(BF16), (((1,), (1,)), ((), ())),
                           preferred_element_type=F32)


def _sigmoid(x):
    return 1.0 / (1.0 + jnp.exp(-x))


def _in_proj_kernel(x_ref, g_ref, w_ref, wl_ref, p_ref, lora_ref, hn_ref):
    @pl.when(pl.program_id(1) == 0)
    def _():
        x = x_ref[...]
        inv = lax.rsqrt(jnp.mean(x * x, axis=-1, keepdims=True) + RMS_EPS)
        hn = (x * inv * g_ref[...]).astype(BF16)
        hn_ref[...] = hn
        lora_ref[...] = jnp.dot(hn, wl_ref[...], preferred_element_type=F32)

    p_ref[...] = jnp.dot(hn_ref[...], w_ref[...], preferred_element_type=F32).astype(p_ref.dtype)


def _in_proj(x2d, g, w_main, w_lora, tm, tn):
    rows = x2d.shape[0]
    cols = w_main.shape[1]
    return pl.pallas_call(
        _in_proj_kernel,
        grid=(rows // tm, cols // tn),
        in_specs=[
            pl.BlockSpec((tm, D_MODEL), lambda i, j: (i, 0)),
            pl.BlockSpec((1, D_MODEL), lambda i, j: (0, 0)),
            pl.BlockSpec((D_MODEL, tn), lambda i, j: (0, j)),
            pl.BlockSpec((D_MODEL, 2 * LORA), lambda i, j: (0, 0)),
        ],
        out_specs=[
            pl.BlockSpec((tm, tn), lambda i, j: (i, j)),
            pl.BlockSpec((tm, 2 * LORA), lambda i, j: (i, 0)),
        ],
        out_shape=[
            jax.ShapeDtypeStruct((rows, cols), BF16),
            jax.ShapeDtypeStruct((rows, 2 * LORA), F32),
        ],
        scratch_shapes=[pltpu.VMEM((tm, D_MODEL), BF16)],
        compiler_params=pltpu.CompilerParams(
            dimension_semantics=("arbitrary", "arbitrary"), vmem_limit_bytes=VMEM_LIMIT),
        name="in_proj",
    )(x2d, g, w_main, w_lora)


def _head_sum(x, e256):
    rows = x.shape[0]
    groups = WIDTH // 256
    stacked = jnp.concatenate([x[:, 256 * g:256 * (g + 1)] for g in range(groups)], axis=0)
    s = _dot(stacked, e256)
    return jnp.concatenate([s[rows * g:rows * (g + 1), :] for g in range(groups)], axis=1)


def _rwkv_kernel(r_ref, k_ref, v_ref, gr_ref, lo_ref, s0_ref, last0_ref, lastl0_ref,
                 mu_ref, mul_ref, w0_ref, a0_ref, kk_ref, ka_ref, rk_ref, gng_ref, gnb_ref,
                 wl_ref, e256_ref, o_ref, s_out, last_out, lastl_out, s_ref, last_ref, lastl_ref,
                 *, n_chunks):
    n_tok = n_chunks * CHUNK

    @pl.when(pl.program_id(1) == 0)
    def _():
        s_ref[...] = s0_ref[...]
        last_ref[...] = last0_ref[...]
        lastl_ref[...] = lastl0_ref[...]

    tok = lax.broadcasted_iota(jnp.int32, (n_tok, 1), 0)

    def shift(cur, last_row, mu):
        prev = jnp.where(tok == 0, last_row, pltpu.roll(cur, 1, 0))
        return cur + (prev - cur) * mu

    r_raw = r_ref[...].astype(F32)
    k_raw = k_ref[...].astype(F32)
    v_raw = v_ref[...].astype(F32)
    lo_raw = lo_ref[...]
    r = shift(r_raw, last_ref[0:1, :], mu_ref[0:1, :])
    k = shift(k_raw, last_ref[1:2, :], mu_ref[1:2, :])
    v = shift(v_raw, last_ref[2:3, :], mu_ref[2:3, :])
    lo = shift(lo_raw, lastl_ref[0:1, :], mul_ref[...])
    last_ref[0:1, :] = r_raw[n_tok - 1:n_tok, :]
    last_ref[1:2, :] = k_raw[n_tok - 1:n_tok, :]
    last_ref[2:3, :] = v_raw[n_tok - 1:n_tok, :]
    lastl_ref[0:1, :] = lo_raw[n_tok - 1:n_tok, :]

    chunk_shift = CHUNK.bit_length() - 1
    ti = lax.broadcasted_iota(jnp.int32, (CHUNK, CHUNK), 0)
    si = lax.broadcasted_iota(jnp.int32, (CHUNK, CHUNK), 1)
    tri = (si <= ti).astype(BF16)
    lane_l = lax.broadcasted_iota(jnp.int32, (CHUNK, 2 * LORA), 1)
    e256 = e256_ref[...]
    w_lora = wl_ref[...]

    row = lax.broadcasted_iota(jnp.int32, (CHUNK, 1), 0)
    lane = lax.broadcasted_iota(jnp.int32, (CHUNK, LANES), 1)
    first = lane < HEAD_DIM
    col = jnp.where(first, lane, lane - HEAD_DIM)
    strict = col < row
    incl = col <= row
    row2 = lax.broadcasted_iota(jnp.int32, (2 * CHUNK, LANES), 0)
    lane2 = lax.broadcasted_iota(jnp.int32, (2 * CHUNK, LANES), 1)
    same_head = (row2 < HEAD_DIM) == (lane2 < HEAD_DIM)
    eye2 = (row2 == lane2).astype(F32)
    pairs = range(PAIRS)

    def blockdiag(t):
        return jnp.concatenate([jnp.where(first, t, 0.0), jnp.where(first, 0.0, t)], axis=0)

    def pair(x, p):
        return x[:, LANES * p:LANES * (p + 1)]

    def prepare(ch, a):
        rows = slice(CHUNK * ch, CHUNK * (ch + 1))
        r_c, k_c, v_c, lo_c = r[rows], k[rows], v[rows], lo[rows]
        pre_w = _dot(jnp.where(lane_l < LORA, jnp.tanh(lo_c), 0.0), w_lora)
        pre_a = _dot(jnp.where(lane_l >= LORA, lo_c, 0.0), w_lora)
        kk = k_c * kk_ref[...]
        kk_sq = _head_sum(kk * kk, e256)
        yield
        logw = -DECAY_SCALE * _sigmoid(w0_ref[...] + pre_w)
        a_lr = _sigmoid(a0_ref[...] + pre_a)
        lw_hi = logw.astype(BF16)
        lw_lo = (logw - lw_hi.astype(F32)).astype(BF16)
        cw = jnp.dot(jnp.concatenate([tri, tri], axis=1), jnp.concatenate([lw_hi, lw_lo], axis=0),
                     preferred_element_type=F32)
        yield
        w_in = jnp.exp(cw)
        w_ex = jnp.exp(cw - logw)
        w_inv = jnp.exp(-cw)
        kappa = kk * lax.rsqrt(kk_sq + L2_EPS)
        k_rep = k_c * (1.0 + (a_lr - 1.0) * ka_ref[...])
        kappa_h = kappa * w_ex
        r_h = r_c * w_in
        k_h = k_rep * w_inv
        b_h = kappa * a_lr * w_inv
        a.update(
            r=r_c, k_rep=k_rep, v=v_c, w_end=w_in[CHUNK - 1:CHUNK, :],
            lhs=[jnp.concatenate([pair(kappa_h, p), pair(r_h, p)], axis=0).astype(BF16)
                 for p in pairs],
            kb=[jnp.concatenate([pair(k_h, p), pair(b_h, p)], axis=0).astype(BF16) for p in pairs],
            kbd=[jnp.concatenate([blockdiag(pair(k_h, p)), blockdiag(pair(b_h, p))],
                                 axis=0).astype(BF16) for p in pairs],
            vbd=[blockdiag(pair(v_c, p)).astype(BF16) for p in pairs])

    def independent(a):
        gram = [_dot_nt(a["lhs"][p], a["kbd"][p]) for p in pairs]
        yield
        l_bd = [blockdiag(jnp.where(strict, g[:CHUNK, LANES:], 0.0)) for g in gram]
        mk = [jnp.concatenate([jnp.where(strict, g[:CHUNK, :LANES], 0.0),
                               jnp.where(incl, g[CHUNK:, :LANES], 0.0)], axis=0) for g in gram]
        a["mb_r"] = [jnp.where(incl, g[CHUNK:, LANES:], 0.0) for g in gram]
        a["mv"] = [_dot(mk[p], a["vbd"][p]) for p in pairs]
        yield
        t_inv = None
        for level in range(chunk_shift):
            size = 1 << level
            rb = jnp.right_shift(row2, level)
            off = ((rb & 1) == 1) & (jnp.right_shift(lane2, level) == rb - 1)
            l_off = [jnp.where(off, l, 0.0) for l in l_bd]
            if level == 0:
                t_inv = [eye2 - l for l in l_off]
            elif size < 8:
                lt = [_dot(l_off[p], t_inv[p]) for p in pairs]
                yield
                t_inv = [t_inv[p] - _dot(t_inv[p], lt[p]) for p in pairs]
                yield
            else:
                odd = [slice(start, start + size) for start in range(size, 2 * CHUNK, 2 * size)]

                def take(x):
                    return jnp.concatenate([x[rows] for rows in odd], axis=0)

                def spread(x_odd, x_even):
                    pieces = []
                    for n, rows in enumerate(odd):
                        pieces += [x_even[rows.start - size:rows.start], x_odd[size * n:size * (n + 1)]]
                    return jnp.concatenate(pieces, axis=0)

                lt_odd = [_dot(take(l_off[p]), t_inv[p]) for p in pairs]
                yield
                t_odd = [take(t_inv[p]) for p in pairs]
                zero = jnp.zeros((2 * CHUNK, LANES), F32)
                delta = [_dot(t_odd[p], spread(lt_odd[p], zero)) for p in pairs]
                yield
                t_inv = [spread(t_odd[p] - delta[p], t_inv[p]) for p in pairs]
        a["t_inv"] = t_inv

    state = [s_ref[p] for p in pairs]

    def dependent(a):
        sp = [_dot_nt(a["lhs"][p], state[p]) for p in pairs]
        yield
        u_bd = [_dot(a["t_inv"][p], blockdiag(sp[p][:CHUNK] + a["mv"][p][:CHUNK])) for p in pairs]
        yield
        vu_t = [jnp.concatenate([pair(a["v"], p), -(u_bd[p][:CHUNK] + u_bd[p][CHUNK:])], axis=0).T
                for p in pairs]
        ds = [_dot(vu_t[p], a["kb"][p]) for p in pairs]
        yd = [_dot(a["mb_r"][p], u_bd[p]) for p in pairs]
        yield
        for p in pairs:
            state[p] = (state[p] + jnp.where(same_head, ds[p], 0.0)) * pair(a["w_end"], p)
        a["y"] = jnp.concatenate(
            [sp[p][CHUNK:] + a["mv"][p][CHUNK:] - yd[p] for p in pairs], axis=1)

    def finish(ch, a):
        y = a["y"]
        mean = _head_sum(y, e256) * (1.0 / HEAD_DIM)
        yield
        d = y - mean
        var = _head_sum(d * d, e256) * (1.0 / HEAD_DIM)
        yield
        y = d * lax.rsqrt(var + GN_EPS) * gng_ref[...] + gnb_ref[...]
        bonus = _head_sum(a["r"] * a["k_rep"] * rk_ref[...], e256) * a["v"]
        yield
        rows = slice(CHUNK * ch, CHUNK * (ch + 1))
        g = gr_ref[rows, :].astype(F32)
        o_ref[rows, :] = ((y + bonus) * (g * _sigmoid(g))).astype(o_ref.dtype)

    chunks = [dict() for _ in range(n_chunks)]
    for ch in range(min(2, n_chunks)):
        for _ in prepare(ch, chunks[ch]):
            pass
    to_prepare = list(range(2, n_chunks))
    waiting = list(range(n_chunks))
    prep, running, finished, dep, dep_ch, tails = None, [], set(), None, 0, []
    while waiting or running or dep is not None or dep_ch < n_chunks or tails:
        if prep is None and to_prepare:
            prep = prepare(to_prepare[0], chunks[to_prepare[0]])
        while waiting and len(running) < 2 and "lhs" in chunks[waiting[0]]:
            ch = waiting.pop(0)
            running.append((ch, independent(chunks[ch])))
        if dep is None and dep_ch in finished:
            dep = dependent(chunks[dep_ch])
        for item in list(running):
            if next(item[1], "done") == "done":
                running.remove(item)
                finished.add(item[0])
        if prep is not None and next(prep, "done") == "done":
            prep = None
            to_prepare.pop(0)
        if dep is not None and next(dep, "done") == "done":
            tails.append(finish(dep_ch, chunks[dep_ch]))
            dep, dep_ch = None, dep_ch + 1
        for gen in list(tails):
            if next(gen, "done") == "done":
                tails.remove(gen)
    for p in pairs:
        s_ref[p] = state[p]
        s_out[p] = state[p]
    last_out[...] = last_ref[...]
    lastl_out[...] = lastl_ref[...]


def _rwkv(p, lora, start, rows_vec, mu_lora, w_lora, e256, batch, seq, n_chunks):
    n_tok = n_chunks * CHUNK
    per_b = seq // n_tok

    def block(seg):
        return lambda b, c: (b * per_b + c, seg)

    def whole(shape):
        return pl.BlockSpec(shape, lambda b, c: (0,) * len(shape))

    state_shapes = [(PAIRS, LANES, LANES), (8, WIDTH), (8, 2 * LORA)]
    in_specs = [
        pl.BlockSpec((n_tok, WIDTH), block(SEG_R)),
        pl.BlockSpec((n_tok, WIDTH), block(SEG_K)),
        pl.BlockSpec((n_tok, WIDTH), block(SEG_V)),
        pl.BlockSpec((n_tok, WIDTH), block(SEG_GR)),
        pl.BlockSpec((n_tok, 2 * LORA), block(0)),
    ] + [whole(s) for s in state_shapes] + [whole((3, WIDTH)), whole((1, 2 * LORA))
    ] + [whole((1, WIDTH))] * 7 + [whole((2 * LORA, WIDTH)), whole((256, 256))]
    mu3, w0, a0, k_k, k_a, r_k, gn_g, gn_b = rows_vec
    return pl.pallas_call(
        functools.partial(_rwkv_kernel, n_chunks=n_chunks),
        grid=(batch, per_b),
        in_specs=in_specs,
        out_specs=[pl.BlockSpec((n_tok, WIDTH), block(0))] + [whole(s) for s in state_shapes],
        out_shape=[jax.ShapeDtypeStruct((batch * seq, WIDTH), BF16)]
        + [jax.ShapeDtypeStruct(s, F32) for s in state_shapes],
        scratch_shapes=[pltpu.VMEM(s, F32) for s in state_shapes],
        compiler_params=pltpu.CompilerParams(
            dimension_semantics=("arbitrary", "arbitrary"), vmem_limit_bytes=VMEM_LIMIT),
        name="rwkv",
    )(p, p, p, p, lora, *start, mu3, mu_lora, w0, a0, k_k, k_a, r_k, gn_g, gn_b, w_lora, e256)


def _sb_kernel(q_ref, k_ref, v_ref, km_ref, vm_ref, g_ref, u_ref, o_ref, *, tq):
    i = pl.program_id(2)
    n_sub = tq // SUB
    u = u_ref[...]
    first = lax.broadcasted_iota(jnp.int32, (SUB, LANES), 1) < HEAD_DIM
    row = lax.broadcasted_iota(jnp.int32, (2 * SUB, SUB), 0)
    lane_k = lax.broadcasted_iota(jnp.int32, (2 * SUB, SUB), 1)
    causal = lane_k < jnp.where(row < SUB, row, row - SUB)
    meta_visible = lax.broadcasted_iota(jnp.int32, (2 * SUB, LANES), 1) < N_META

    def stacked_q(s):
        q = q_ref[SUB * s:SUB * (s + 1), :].astype(F32) * (HEAD_DIM ** -0.5)
        return jnp.concatenate([jnp.where(first, q, 0.0), jnp.where(first, 0.0, q)],
                               axis=0).astype(BF16)

    def cores(items):
        zs = [_dot_nt(qs, kb) for qs, kb, _, _ in items]
        sps = []
        for z, (_, _, _, visible) in zip(zs, items):
            sp = jnp.maximum(z, 0.0) + jnp.log(1.0 + jnp.exp2(jnp.abs(z) * (-LOG2E)))
            sps.append(sp if visible is None else jnp.where(visible, sp, 0.0))
        css = [jnp.dot(sp.astype(BF16), u[:sp.shape[1], :sp.shape[1]], preferred_element_type=F32)
               for sp in sps]
        out = []
        for z, sp, cs, (_, _, vb, visible) in zip(zs, sps, css, items):
            w = jnp.exp(z - sp - cs)
            if visible is not None:
                w = jnp.where(visible, w, 0.0)
            wcat = jnp.concatenate([w[:SUB], w[SUB:]], axis=1).astype(BF16)
            first_v = lax.broadcasted_iota(jnp.int32, vb.shape, 1) < HEAD_DIM
            vbd = jnp.concatenate([jnp.where(first_v, vb, 0), jnp.where(first_v, 0, vb)], axis=0)
            out.append((jnp.dot(wcat, vbd, preferred_element_type=F32), cs[:, 0:1] + sp[:, 0:1]))
        return out

    def nearer(carry):
        f = jnp.exp(-carry)
        return jnp.where(first, f[:SUB], f[SUB:])

    def keys(j):
        start = pl.multiple_of(j * SUB, SUB)
        return k_ref[pl.ds(start, SUB), :], v_ref[pl.ds(start, SUB), :]

    qss = [stacked_q(s) for s in range(n_sub)]
    items = []
    for s in range(n_sub):
        ii = i * n_sub + s
        items.append((qss[s],) + keys(ii) + (causal,))
        items.append((qss[s],) + keys(jnp.maximum(ii - 1, 0)) + (None,))
    first_two = cores(items)

    carries, accs = [], []
    for s in range(n_sub):
        (pv_d, tot_d), (pv_p, tot_p) = first_two[2 * s], first_two[2 * s + 1]
        has_prev = (i * n_sub + s > 0).astype(F32)
        accs.append(pv_d + pv_p * (nearer(tot_d) * has_prev))
        carries.append(tot_d + tot_p * has_prev)

    def alive(carry):
        return jnp.min(carry) <= DEAD_CARRY

    def farther(*operands):
        out = []
        for s in range(n_sub):
            carry, acc = operands[s], operands[n_sub + s]

            def body(state, s=s):
                j, _, carry, acc = state
                (pv, tot), = cores([(qss[s],) + keys(j) + (None,)])
                carry_new = carry + tot
                return j - 1, alive(carry_new), carry_new, acc + pv * nearer(carry)

            _, still, carry, acc = lax.while_loop(
                lambda state: (state[0] >= 0) & state[1], body,
                (i * n_sub + s - 2, alive(carry), carry, acc))

            def with_meta(carry, acc, s=s):
                (pv, _), = cores([(qss[s], km_ref[...], vm_ref[...], meta_visible)])
                return acc + pv * nearer(carry)

            out.append(lax.cond(still, with_meta, lambda carry, acc: acc, carry, acc))
        return tuple(out)

    any_alive = functools.reduce(jnp.logical_or, [alive(carry) for carry in carries])
    accs = lax.cond(any_alive, farther, lambda *operands: tuple(operands[n_sub:]), *carries, *accs)
    for s in range(n_sub):
        g = g_ref[SUB * s:SUB * (s + 1), :].astype(F32)
        o_ref[SUB * s:SUB * (s + 1), :] = (accs[s] * (g * _sigmoid(g))).astype(o_ref.dtype)


def _sb_attn(p_main, k_meta, v_meta, u, batch, seq, tq):
    nq = seq // tq

    def seq_map(seg):
        return lambda b, p, i: (b, seg * PAIRS + p)

    def tile_map(seg):
        return lambda b, p, i: (b * nq + i, seg * PAIRS + p)

    return pl.pallas_call(
        functools.partial(_sb_kernel, tq=tq),
        grid=(batch, PAIRS, nq),
        in_specs=[
            pl.BlockSpec((tq, LANES), tile_map(SEG_Q)),
            pl.BlockSpec((seq, LANES), seq_map(SEG_KS)),
            pl.BlockSpec((seq, LANES), seq_map(SEG_VS)),
            pl.BlockSpec((LANES, LANES), lambda b, p, i: (0, p)),
            pl.BlockSpec((LANES, LANES), lambda b, p, i: (0, p)),
            pl.BlockSpec((tq, LANES), tile_map(SEG_GS)),
            pl.BlockSpec((SUB, SUB), lambda b, p, i: (0, 0)),
        ],
        out_specs=pl.BlockSpec((tq, LANES), lambda b, p, i: (b * nq + i, p)),
        out_shape=jax.ShapeDtypeStruct((batch * seq, WIDTH), BF16),
        compiler_params=pltpu.CompilerParams(
            dimension_semantics=("arbitrary", "arbitrary", "arbitrary"),
            vmem_limit_bytes=VMEM_LIMIT),
        name="sb_attn",
    )(p_main, p_main, p_main, k_meta, v_meta, p_main, u)


def _out_kernel(yr_ref, ys_ref, mr_ref, ms_ref, x_ref, wr_ref, ws_ref, wo_ref, g_ref, o_ref):
    pr = jnp.dot(yr_ref[...], wr_ref[...], preferred_element_type=F32)
    ps = jnp.dot(ys_ref[...], ws_ref[...], preferred_element_type=F32)
    mixed = (_sigmoid(mr_ref[...].astype(F32)) * pr + _sigmoid(ms_ref[...].astype(F32)) * ps)
    o = jnp.dot(mixed.astype(BF16), wo_ref[...], preferred_element_type=F32)
    inv = lax.rsqrt(jnp.mean(o * o, axis=-1, keepdims=True) + RMS_EPS)
    o_ref[...] = x_ref[...] + o * inv * g_ref[...]


def _out_proj(y_rwkv, y_sb, p_main, x2d, w_r, w_s, w_o, g, tm):
    rows = x2d.shape[0]
    w_spec = pl.BlockSpec((WIDTH, D_MODEL), lambda i: (0, 0))
    return pl.pallas_call(
        _out_kernel,
        grid=(rows // tm,),
        in_specs=[
            pl.BlockSpec((tm, WIDTH), lambda i: (i, 0)),
            pl.BlockSpec((tm, WIDTH), lambda i: (i, 0)),
            pl.BlockSpec((tm, WIDTH), lambda i: (i, SEG_MR)),
            pl.BlockSpec((tm, WIDTH), lambda i: (i, SEG_MS)),
            pl.BlockSpec((tm, D_MODEL), lambda i: (i, 0)),
            w_spec, w_spec, w_spec,
            pl.BlockSpec((1, D_MODEL), lambda i: (0, 0)),
        ],
        out_specs=pl.BlockSpec((tm, D_MODEL), lambda i: (i, 0)),
        out_shape=jax.ShapeDtypeStruct((rows, D_MODEL), F32),
        compiler_params=pltpu.CompilerParams(
            dimension_semantics=("arbitrary",), vmem_limit_bytes=VMEM_LIMIT),
        name="out_proj",
    )(y_rwkv, y_sb, p_main, p_main, x2d, w_r, w_s, w_o, g)


def _tiles(batch, seq):
    rows = batch * seq
    tm_in = min(1024, rows)
    tm_out = min(512, rows)
    tq = min(2048, seq)
    return tm_in, 2560, tm_out, tq


def kernel(x, meta_tokens, pre_norm_g, post_norm_g, w_in, rwkv_mu, rwkv_w0, rwkv_w_up, rwkv_a0,
           rwkv_a_up, rwkv_k_k, rwkv_k_a, rwkv_r_k, rwkv_gn_g, rwkv_gn_b, w_proj_rwkv, w_proj_sb,
           w_out):
    batch, seq, d_model = x.shape
    assert d_model == D_MODEL and w_in.shape[0] == 1, "single-layer kernel"
    assert seq % 512 == 0 and meta_tokens.shape == (N_META, D_MODEL)
    tm_in, tn_in, tm_out, tq = _tiles(batch, seq)

    w = w_in[0].astype(BF16)
    w_main = jnp.concatenate([w[:, :3 * WIDTH], w[:, 3 * WIDTH + 2 * LORA:]], axis=1)
    w_lo = w[:, 3 * WIDTH:3 * WIDTH + 2 * LORA]
    g_pre = pre_norm_g[0][None, :]

    x2d = x.reshape(batch * seq, D_MODEL)
    p_main, lora_main = _in_proj(x2d, g_pre, w_main, w_lo, tm_in, tn_in)
    meta_rows = jnp.zeros((CHUNK, D_MODEL), F32).at[CHUNK - N_META:].set(meta_tokens.astype(F32))
    p_meta, lora_meta = _in_proj(meta_rows, g_pre, w_main, w_lo, CHUNK, tn_in)

    mu = rwkv_mu[0]
    mu3 = mu[:3 * WIDTH].reshape(3, WIDTH)
    mu_lora = mu[3 * WIDTH:][None, :]
    as_row = lambda t: t[0].reshape(1, WIDTH)
    rows_vec = (mu3, as_row(rwkv_w0), as_row(rwkv_a0), as_row(rwkv_k_k), as_row(rwkv_k_a),
                as_row(rwkv_r_k), as_row(rwkv_gn_g), as_row(rwkv_gn_b))
    w_lora_up = jnp.concatenate([rwkv_w_up[0], rwkv_a_up[0]], axis=0).astype(BF16)
    idx = jnp.arange(256)
    e256 = (idx[:, None] // HEAD_DIM == idx[None, :] // HEAD_DIM).astype(BF16)
    zero_start = (jnp.zeros((PAIRS, LANES, LANES), F32), jnp.zeros((8, WIDTH), F32),
                  jnp.zeros((8, 2 * LORA), F32))
    _, *after_meta = _rwkv(p_meta, lora_meta, zero_start, rows_vec, mu_lora, w_lora_up, e256,
                           1, CHUNK, 1)
    y_rwkv, *_ = _rwkv(p_main, lora_main, after_meta, rows_vec, mu_lora, w_lora_up, e256,
                       batch, seq, RW_CHUNKS)

    kidx = jnp.arange(SUB)
    u_tri = (kidx[:, None] > kidx[None, :]).astype(BF16)
    pad = jnp.zeros((LANES - N_META, WIDTH), BF16)
    meta_seg = lambda seg: jnp.concatenate(
        [p_meta[CHUNK - N_META:, seg * WIDTH:(seg + 1) * WIDTH], pad], axis=0)
    y_sb = _sb_attn(p_main, meta_seg(SEG_KS), meta_seg(SEG_VS), u_tri, batch, seq, tq)

    out = _out_proj(y_rwkv, y_sb, p_main, x2d, w_proj_rwkv[0].astype(BF16),
                    w_proj_sb[0].astype(BF16), w_out[0].astype(BF16), post_norm_g[0][None, :],
                    tm_out)
    return out.reshape(batch, seq, D_MODEL)
```

```python
import functools

import jax
import jax.numpy as jnp
from jax import lax
from jax.experimental import pallas as pl
from jax.experimental.pallas import tpu as pltpu

F32 = jnp.float32
BF16 = jnp.bfloat16

D_MODEL = 1024
N_META = 16
HEADS = 16
HEAD_DIM = 64
WIDTH = HEADS * HEAD_DIM
LORA = 64
RMS_EPS = 1e-6
GN_EPS = 64e-5
L2_EPS = 1e-12
DECAY_SCALE = 0.6065306597126334
LOG2E = 1.4426950408889634

LANES = 128
PAIRS = WIDTH // LANES
CHUNK = 64
RW_CHUNKS = 8
SUB = 256
DEAD_CARRY = 110.0
N_SEG = 10
SEG_R, SEG_K, SEG_V, SEG_GR, SEG_Q, SEG_KS, SEG_VS, SEG_GS, SEG_MR, SEG_MS = range(N_SEG)
VMEM_LIMIT = 56 * 1024 * 1024


def _dot(a, b):
    return jnp.dot(a.astype(BF16), b.astype(BF16), preferred_element_type=F32)


def _dot_nt(a, b):
    return lax.dot_general(a.astype(BF16), b.astype(BF16), (((1,), (1,)), ((), ())),
                           preferred_element_type=F32)


def _sigmoid(x):
    return 1.0 / (1.0 + jnp.exp(-x))


def _in_proj_kernel(x_ref, g_ref, w_ref, wl_ref, p_ref, lora_ref, hn_ref):
    @pl.when(pl.program_id(1) == 0)
    def _():
        x = x_ref[...]
        inv = lax.rsqrt(jnp.mean(x * x, axis=-1, keepdims=True) + RMS_EPS)
        hn = (x * inv * g_ref[...]).astype(BF16)
        hn_ref[...] = hn
        lora_ref[...] = jnp.dot(hn, wl_ref[...], preferred_element_type=F32)

    p_ref[...] = jnp.dot(hn_ref[...], w_ref[...], preferred_element_type=F32).astype(p_ref.dtype)


def _in_proj(x2d, g, w_main, w_lora, tm, tn):
    rows = x2d.shape[0]
    cols = w_main.shape[1]
    return pl.pallas_call(
        _in_proj_kernel,
        grid=(rows // tm, cols // tn),
        in_specs=[
            pl.BlockSpec((tm, D_MODEL), lambda i, j: (i, 0)),
            pl.BlockSpec((1, D_MODEL), lambda i, j: (0, 0)),
            pl.BlockSpec((D_MODEL, tn), lambda i, j: (0, j)),
            pl.BlockSpec((D_MODEL, 2 * LORA), lambda i, j: (0, 0)),
        ],
        out_specs=[
            pl.BlockSpec((tm, tn), lambda i, j: (i, j)),
            pl.BlockSpec((tm, 2 * LORA), lambda i, j: (i, 0)),
        ],
        out_shape=[
            jax.ShapeDtypeStruct((rows, cols), BF16),
            jax.ShapeDtypeStruct((rows, 2 * LORA), F32),
        ],
        scratch_shapes=[pltpu.VMEM((tm, D_MODEL), BF16)],
        compiler_params=pltpu.CompilerParams(
            dimension_semantics=("arbitrary", "arbitrary"), vmem_limit_bytes=VMEM_LIMIT),
        name="in_proj",
    )(x2d, g, w_main, w_lora)


def _head_sum(x, e256):
    rows = x.shape[0]
    groups = WIDTH // 256
    stacked = jnp.concatenate([x[:, 256 * g:256 * (g + 1)] for g in range(groups)], axis=0)
    s = _dot(stacked, e256)
    return jnp.concatenate([s[rows * g:rows * (g + 1), :] for g in range(groups)], axis=1)


def _rwkv_kernel(r_ref, k_ref, v_ref, gr_ref, lo_ref, s0_ref, last0_ref, lastl0_ref,
                 mu_ref, mul_ref, w0_ref, a0_ref, kk_ref, ka_ref, rk_ref, gng_ref, gnb_ref,
                 wl_ref, e256_ref, o_ref, s_out, last_out, lastl_out, s_ref, last_ref, lastl_ref,
                 *, n_chunks):
    n_tok = n_chunks * CHUNK

    @pl.when(pl.program_id(1) == 0)
    def _():
        s_ref[...] = s0_ref[...]
        last_ref[...] = last0_ref[...]
        lastl_ref[...] = lastl0_ref[...]

    tok = lax.broadcasted_iota(jnp.int32, (n_tok, 1), 0)

    def shift(cur, last_row, mu):
        prev = jnp.where(tok == 0, last_row, pltpu.roll(cur, 1, 0))
        return cur + (prev - cur) * mu

    r_raw = r_ref[...].astype(F32)
    k_raw = k_ref[...].astype(F32)
    v_raw = v_ref[...].astype(F32)
    lo_raw = lo_ref[...]
    r = shift(r_raw, last_ref[0:1, :], mu_ref[0:1, :])
    k = shift(k_raw, last_ref[1:2, :], mu_ref[1:2, :])
    v = shift(v_raw, last_ref[2:3, :], mu_ref[2:3, :])
    lo = shift(lo_raw, lastl_ref[0:1, :], mul_ref[...])
    last_ref[0:1, :] = r_raw[n_tok - 1:n_tok, :]
    last_ref[1:2, :] = k_raw[n_tok - 1:n_tok, :]
    last_ref[2:3, :] = v_raw[n_tok - 1:n_tok, :]
    lastl_ref[0:1, :] = lo_raw[n_tok - 1:n_tok, :]

    chunk_shift = CHUNK.bit_length() - 1
    ti = lax.broadcasted_iota(jnp.int32, (CHUNK, CHUNK), 0)
    si = lax.broadcasted_iota(jnp.int32, (CHUNK, CHUNK), 1)
    tri = (si <= ti).astype(BF16)
    lane_l = lax.broadcasted_iota(jnp.int32, (CHUNK, 2 * LORA), 1)
    e256 = e256_ref[...]
    w_lora = wl_ref[...]

    row = lax.broadcasted_iota(jnp.int32, (CHUNK, 1), 0)
    lane = lax.broadcasted_iota(jnp.int32, (CHUNK, LANES), 1)
    first = lane < HEAD_DIM
    col = jnp.where(first, lane, lane - HEAD_DIM)
    strict = col < row
    incl = col <= row
    row2 = lax.broadcasted_iota(jnp.int32, (2 * CHUNK, LANES), 0)
    lane2 = lax.broadcasted_iota(jnp.int32, (2 * CHUNK, LANES), 1)
    same_head = (row2 < HEAD_DIM) == (lane2 < HEAD_DIM)
    eye2 = (row2 == lane2).astype(F32)
    pairs = range(PAIRS)

    def blockdiag(t):
        return jnp.concatenate([jnp.where(first, t, 0.0), jnp.where(first, 0.0, t)], axis=0)

    def pair(x, p):
        return x[:, LANES * p:LANES * (p + 1)]

    def prepare(ch, a):
        rows = slice(CHUNK * ch, CHUNK * (ch + 1))
        r_c, k_c, v_c, lo_c = r[rows], k[rows], v[rows], lo[rows]
        pre_w = _dot(jnp.where(lane_l < LORA, jnp.tanh(lo_c), 0.0), w_lora)
        pre_a = _dot(jnp.where(lane_l >= LORA, lo_c, 0.0), w_lora)
        kk = k_c * kk_ref[...]
        kk_sq = _head_sum(kk * kk, e256)
        yield
        logw = -DECAY_SCALE * _sigmoid(w0_ref[...] + pre_w)
        a_lr = _sigmoid(a0_ref[...] + pre_a)
        lw_hi = logw.astype(BF16)
        lw_lo = (logw - lw_hi.astype(F32)).astype(BF16)
        cw = jnp.dot(jnp.concatenate([tri, tri], axis=1), jnp.concatenate([lw_hi, lw_lo], axis=0),
                     preferred_element_type=F32)
        yield
        w_in = jnp.exp(cw)
        w_ex = jnp.exp(cw - logw)
        w_inv = jnp.exp(-cw)
        kappa = kk * lax.rsqrt(kk_sq + L2_EPS)
        k_rep = k_c * (1.0 + (a_lr - 1.0) * ka_ref[...])
        kappa_h = kappa * w_ex
        r_h = r_c * w_in
        k_h = k_rep * w_inv
        b_h = kappa * a_lr * w_inv
        a.update(
            r=r_c, k_rep=k_rep, v=v_c, w_end=w_in[CHUNK - 1:CHUNK, :],
            lhs=[jnp.concatenate([pair(kappa_h, p), pair(r_h, p)], axis=0).astype(BF16)
                 for p in pairs],
            kb=[jnp.concatenate([pair(k_h, p), pair(b_h, p)], axis=0).astype(BF16) for p in pairs],
            kbd=[jnp.concatenate([blockdiag(pair(k_h, p)), blockdiag(pair(b_h, p))],
                                 axis=0).astype(BF16) for p in pairs],
            vbd=[blockdiag(pair(v_c, p)).astype(BF16) for p in pairs])

    def independent(a):
        gram = [_dot_nt(a["lhs"][p], a["kbd"][p]) for p in pairs]
        yield
        l_bd = [blockdiag(jnp.where(strict, g[:CHUNK, LANES:], 0.0)) for g in gram]
        mk = [jnp.concatenate([jnp.where(strict, g[:CHUNK, :LANES], 0.0),
                               jnp.where(incl, g[CHUNK:, :LANES], 0.0)], axis=0) for g in gram]
        a["mb_r"] = [jnp.where(incl, g[CHUNK:, LANES:], 0.0) for g in gram]
        a["mv"] = [_dot(mk[p], a["vbd"][p]) for p in pairs]
        yield
        t_inv = None
        for level in range(chunk_shift):
            size = 1 << level
            rb = jnp.right_shift(row2, level)
            off = ((rb & 1) == 1) & (jnp.right_shift(lane2, level) == rb - 1)
            l_off = [jnp.where(off, l, 0.0) for l in l_bd]
            if level == 0:
                t_inv = [eye2 - l for l in l_off]
            elif size < 8:
                lt = [_dot(l_off[p], t_inv[p]) for p in pairs]
                yield
                t_inv = [t_inv[p] - _dot(t_inv[p], lt[p]) for p in pairs]
                yield
            else:
                odd = [slice(start, start + size) for start in range(size, 2 * CHUNK, 2 * size)]

                def take(x):
                    return jnp.concatenate([x[rows] for rows in odd], axis=0)

                def spread(x_odd, x_even):
                    pieces = []
                    for n, rows in enumerate(odd):
                        pieces += [x_even[rows.start - size:rows.start], x_odd[size * n:size * (n + 1)]]
                    return jnp.concatenate(pieces, axis=0)

                lt_odd = [_dot(take(l_off[p]), t_inv[p]) for p in pairs]
                yield
                t_odd = [take(t_inv[p]) for p in pairs]
                zero = jnp.zeros((2 * CHUNK, LANES), F32)
                delta = [_dot(t_odd[p], spread(lt_odd[p], zero)) for p in pairs]
                yield
                t_inv = [spread(t_odd[p] - delta[p], t_inv[p]) for p in pairs]
        a["t_inv"] = t_inv

    state = [s_ref[p] for p in pairs]

    def dependent(a):
        sp = [_dot_nt(a["lhs"][p], state[p]) for p in pairs]
        yield
        u_bd = [_dot(a["t_inv"][p], blockdiag(sp[p][:CHUNK] + a["mv"][p][:CHUNK])) for p in pairs]
        yield
        vu_t = [jnp.concatenate([pair(a["v"], p), -(u_bd[p][:CHUNK] + u_bd[p][CHUNK:])], axis=0).T
                for p in pairs]
        ds = [_dot(vu_t[p], a["kb"][p]) for p in pairs]
        yd = [_dot(a["mb_r"][p], u_bd[p]) for p in pairs]
        yield
        for p in pairs:
            state[p] = (state[p] + jnp.where(same_head, ds[p], 0.0)) * pair(a["w_end"], p)
        a["y"] = jnp.concatenate(
            [sp[p][CHUNK:] + a["mv"][p][CHUNK:] - yd[p] for p in pairs], axis=1)

    def finish(ch, a):
        y = a["y"]
        mean = _head_sum(y, e256) * (1.0 / HEAD_DIM)
        yield
        d = y - mean
        var = _head_sum(d * d, e256) * (1.0 / HEAD_DIM)
        yield
        y = d * lax.rsqrt(var + GN_EPS) * gng_ref[...] + gnb_ref[...]
        bonus = _head_sum(a["r"] * a["k_rep"] * rk_ref[...], e256) * a["v"]
        yield
        rows = slice(CHUNK * ch, CHUNK * (ch + 1))
        g = gr_ref[rows, :].astype(F32)
        o_ref[rows, :] = ((y + bonus) * (g * _sigmoid(g))).astype(o_ref.dtype)

    chunks = [dict() for _ in range(n_chunks)]
    for ch in range(min(2, n_chunks)):
        for _ in prepare(ch, chunks[ch]):
            pass
    to_prepare = list(range(2, n_chunks))
    waiting = list(range(n_chunks))
    prep, running, finished, dep, dep_ch, tails = None, [], set(), None, 0, []
    while waiting or running or dep is not None or dep_ch < n_chunks or tails:
        if prep is None and to_prepare:
            prep = prepare(to_prepare[0], chunks[to_prepare[0]])
        while waiting and len(running) < 2 and "lhs" in chunks[waiting[0]]:
            ch = waiting.pop(0)
            running.append((ch, independent(chunks[ch])))
        if dep is None and dep_ch in finished:
            dep = dependent(chunks[dep_ch])
        for item in list(running):
            if next(item[1], "done") == "done":
                running.remove(item)
                finished.add(item[0])
        if prep is not None and next(prep, "done") == "done":
            prep = None
            to_prepare.pop(0)
        if dep is not None and next(dep, "done") == "done":
            tails.append(finish(dep_ch, chunks[dep_ch]))
            dep, dep_ch = None, dep_ch + 1
        for gen in list(tails):
            if next(gen, "done") == "done":
                tails.remove(gen)
    for p in pairs:
        s_ref[p] = state[p]
        s_out[p] = state[p]
    last_out[...] = last_ref[...]
    lastl_out[...] = lastl_ref[...]


def _rwkv(p, lora, start, rows_vec, mu_lora, w_lora, e256, batch, seq, n_chunks):
    n_tok = n_chunks * CHUNK
    per_b = seq // n_tok

    def block(seg):
        return lambda b, c: (b * per_b + c, seg)

    def whole(shape):
        return pl.BlockSpec(shape, lambda b, c: (0,) * len(shape))

    state_shapes = [(PAIRS, LANES, LANES), (8, WIDTH), (8, 2 * LORA)]
    in_specs = [
        pl.BlockSpec((n_tok, WIDTH), block(SEG_R)),
        pl.BlockSpec((n_tok, WIDTH), block(SEG_K)),
        pl.BlockSpec((n_tok, WIDTH), block(SEG_V)),
        pl.BlockSpec((n_tok, WIDTH), block(SEG_GR)),
        pl.BlockSpec((n_tok, 2 * LORA), block(0)),
    ] + [whole(s) for s in state_shapes] + [whole((3, WIDTH)), whole((1, 2 * LORA))
    ] + [whole((1, WIDTH))] * 7 + [whole((2 * LORA, WIDTH)), whole((256, 256))]
    mu3, w0, a0, k_k, k_a, r_k, gn_g, gn_b = rows_vec
    return pl.pallas_call(
        functools.partial(_rwkv_kernel, n_chunks=n_chunks),
        grid=(batch, per_b),
        in_specs=in_specs,
        out_specs=[pl.BlockSpec((n_tok, WIDTH), block(0))] + [whole(s) for s in state_shapes],
        out_shape=[jax.ShapeDtypeStruct((batch * seq, WIDTH), BF16)]
        + [jax.ShapeDtypeStruct(s, F32) for s in state_shapes],
        scratch_shapes=[pltpu.VMEM(s, F32) for s in state_shapes],
        compiler_params=pltpu.CompilerParams(
            dimension_semantics=("arbitrary", "arbitrary"), vmem_limit_bytes=VMEM_LIMIT),
        name="rwkv",
    )(p, p, p, p, lora, *start, mu3, mu_lora, w0, a0, k_k, k_a, r_k, gn_g, gn_b, w_lora, e256)


def _sb_kernel(q_ref, k_ref, v_ref, km_ref, vm_ref, g_ref, u_ref, o_ref, *, tq):
    i = pl.program_id(2)
    n_sub = tq // SUB
    u = u_ref[...]
    first = lax.broadcasted_iota(jnp.int32, (SUB, LANES), 1) < HEAD_DIM
    row = lax.broadcasted_iota(jnp.int32, (2 * SUB, SUB), 0)
    lane_k = lax.broadcasted_iota(jnp.int32, (2 * SUB, SUB), 1)
    causal = lane_k < jnp.where(row < SUB, row, row - SUB)
    meta_visible = lax.broadcasted_iota(jnp.int32, (2 * SUB, LANES), 1) < N_META

    def stacked_q(s):
        q = q_ref[SUB * s:SUB * (s + 1), :].astype(F32) * (HEAD_DIM ** -0.5)
        return jnp.concatenate([jnp.where(first, q, 0.0), jnp.where(first, 0.0, q)],
                               axis=0).astype(BF16)

    def cores(items):
        zs = [_dot_nt(qs, kb) for qs, kb, _, _ in items]
        sps = []
        for z, (_, _, _, visible) in zip(zs, items):
            sp = jnp.maximum(z, 0.0) + jnp.log(1.0 + jnp.exp2(jnp.abs(z) * (-LOG2E)))
            sps.append(sp if visible is None else jnp.where(visible, sp, 0.0))
        css = [jnp.dot(sp.astype(BF16), u[:sp.shape[1], :sp.shape[1]], preferred_element_type=F32)
               for sp in sps]
        out = []
        for z, sp, cs, (_, _, vb, visible) in zip(zs, sps, css, items):
            w = jnp.exp(z - sp - cs)
            if visible is not None:
                w = jnp.where(visible, w, 0.0)
            wcat = jnp.concatenate([w[:SUB], w[SUB:]], axis=1).astype(BF16)
            first_v = lax.broadcasted_iota(jnp.int32, vb.shape, 1) < HEAD_DIM
            vbd = jnp.concatenate([jnp.where(first_v, vb, 0), jnp.where(first_v, 0, vb)], axis=0)
            out.append((jnp.dot(wcat, vbd, preferred_element_type=F32), cs[:, 0:1] + sp[:, 0:1]))
        return out

    def nearer(carry):
        f = jnp.exp(-carry)
        return jnp.where(first, f[:SUB], f[SUB:])

    def keys(j):
        start = pl.multiple_of(j * SUB, SUB)
        return k_ref[pl.ds(start, SUB), :], v_ref[pl.ds(start, SUB), :]

    qss = [stacked_q(s) for s in range(n_sub)]
    items = []
    for s in range(n_sub):
        ii = i * n_sub + s
        items.append((qss[s],) + keys(ii) + (causal,))
        items.append((qss[s],) + keys(jnp.maximum(ii - 1, 0)) + (None,))
    first_two = cores(items)

    carries, accs = [], []
    for s in range(n_sub):
        (pv_d, tot_d), (pv_p, tot_p) = first_two[2 * s], first_two[2 * s + 1]
        has_prev = (i * n_sub + s > 0).astype(F32)
        accs.append(pv_d + pv_p * (nearer(tot_d) * has_prev))
        carries.append(tot_d + tot_p * has_prev)

    def alive(carry):
        return jnp.min(carry) <= DEAD_CARRY

    def farther(*operands):
        out = []
        for s in range(n_sub):
            carry, acc = operands[s], operands[n_sub + s]

            def body(state, s=s):
                j, _, carry, acc = state
                (pv, tot), = cores([(qss[s],) + keys(j) + (None,)])
                carry_new = carry + tot
                return j - 1, alive(carry_new), carry_new, acc + pv * nearer(carry)

            _, still, carry, acc = lax.while_loop(
                lambda state: (state[0] >= 0) & state[1], body,
                (i * n_sub + s - 2, alive(carry), carry, acc))

            def with_meta(carry, acc, s=s):
                (pv, _), = cores([(qss[s], km_ref[...], vm_ref[...], meta_visible)])
                return acc + pv * nearer(carry)

            out.append(lax.cond(still, with_meta, lambda carry, acc: acc, carry, acc))
        return tuple(out)

    any_alive = functools.reduce(jnp.logical_or, [alive(carry) for carry in carries])
    accs = lax.cond(any_alive, farther, lambda *operands: tuple(operands[n_sub:]), *carries, *accs)
    for s in range(n_sub):
        g = g_ref[SUB * s:SUB * (s + 1), :].astype(F32)
        o_ref[SUB * s:SUB * (s + 1), :] = (accs[s] * (g * _sigmoid(g))).astype(o_ref.dtype)


def _sb_attn(p_main, k_meta, v_meta, u, batch, seq, tq):
    nq = seq // tq

    def seq_map(seg):
        return lambda b, p, i: (b, seg * PAIRS + p)

    def tile_map(seg):
        return lambda b, p, i: (b * nq + i, seg * PAIRS + p)

    return pl.pallas_call(
        functools.partial(_sb_kernel, tq=tq),
        grid=(batch, PAIRS, nq),
        in_specs=[
            pl.BlockSpec((tq, LANES), tile_map(SEG_Q)),
            pl.BlockSpec((seq, LANES), seq_map(SEG_KS)),
            pl.BlockSpec((seq, LANES), seq_map(SEG_VS)),
            pl.BlockSpec((LANES, LANES), lambda b, p, i: (0, p)),
            pl.BlockSpec((LANES, LANES), lambda b, p, i: (0, p)),
            pl.BlockSpec((tq, LANES), tile_map(SEG_GS)),
            pl.BlockSpec((SUB, SUB), lambda b, p, i: (0, 0)),
        ],
        out_specs=pl.BlockSpec((tq, LANES), lambda b, p, i: (b * nq + i, p)),
        out_shape=jax.ShapeDtypeStruct((batch * seq, WIDTH), BF16),
        compiler_params=pltpu.CompilerParams(
            dimension_semantics=("arbitrary", "arbitrary", "arbitrary"),
            vmem_limit_bytes=VMEM_LIMIT),
        name="sb_attn",
    )(p_main, p_main, p_main, k_meta, v_meta, p_main, u)


def _out_kernel(yr_ref, ys_ref, mr_ref, ms_ref, x_ref, wr_ref, ws_ref, wo_ref, g_ref, o_ref):
    pr = jnp.dot(yr_ref[...], wr_ref[...], preferred_element_type=F32)
    ps = jnp.dot(ys_ref[...], ws_ref[...], preferred_element_type=F32)
    mixed = (_sigmoid(mr_ref[...].astype(F32)) * pr + _sigmoid(ms_ref[...].astype(F32)) * ps)
    o = jnp.dot(mixed.astype(BF16), wo_ref[...], preferred_element_type=F32)
    inv = lax.rsqrt(jnp.mean(o * o, axis=-1, keepdims=True) + RMS_EPS)
    o_ref[...] = x_ref[...] + o * inv * g_ref[...]


def _out_proj(y_rwkv, y_sb, p_main, x2d, w_r, w_s, w_o, g, tm):
    rows = x2d.shape[0]
    w_spec = pl.BlockSpec((WIDTH, D_MODEL), lambda i: (0, 0))
    return pl.pallas_call(
        _out_kernel,
        grid=(rows // tm,),
        in_specs=[
            pl.BlockSpec((tm, WIDTH), lambda i: (i, 0)),
            pl.BlockSpec((tm, WIDTH), lambda i: (i, 0)),
            pl.BlockSpec((tm, WIDTH), lambda i: (i, SEG_MR)),
            pl.BlockSpec((tm, WIDTH), lambda i: (i, SEG_MS)),
            pl.BlockSpec((tm, D_MODEL), lambda i: (i, 0)),
            w_spec, w_spec, w_spec,
            pl.BlockSpec((1, D_MODEL), lambda i: (0, 0)),
        ],
        out_specs=pl.BlockSpec((tm, D_MODEL), lambda i: (i, 0)),
        out_shape=jax.ShapeDtypeStruct((rows, D_MODEL), F32),
        compiler_params=pltpu.CompilerParams(
            dimension_semantics=("arbitrary",), vmem_limit_bytes=VMEM_LIMIT),
        name="out_proj",
    )(y_rwkv, y_sb, p_main, p_main, x2d, w_r, w_s, w_o, g)


def _tiles(batch, seq):
    rows = batch * seq
    tm_in = min(1024, rows)
    tm_out = min(512, rows)
    tq = min(2048, seq)
    return tm_in, 2560, tm_out, tq


def kernel(x, meta_tokens, pre_norm_g, post_norm_g, w_in, rwkv_mu, rwkv_w0, rwkv_w_up, rwkv_a0,
           rwkv_a_up, rwkv_k_k, rwkv_k_a, rwkv_r_k, rwkv_gn_g, rwkv_gn_b, w_proj_rwkv, w_proj_sb,
           w_out):
    batch, seq, d_model = x.shape
    assert d_model == D_MODEL and w_in.shape[0] == 1, "single-layer kernel"
    assert seq % 512 == 0 and meta_tokens.shape == (N_META, D_MODEL)
    tm_in, tn_in, tm_out, tq = _tiles(batch, seq)

    w = w_in[0].astype(BF16)
    w_main = jnp.concatenate([w[:, :3 * WIDTH], w[:, 3 * WIDTH + 2 * LORA:]], axis=1)
    w_lo = w[:, 3 * WIDTH:3 * WIDTH + 2 * LORA]
    g_pre = pre_norm_g[0][None, :]

    x2d = x.reshape(batch * seq, D_MODEL)
    p_main, lora_main = _in_proj(x2d, g_pre, w_main, w_lo, tm_in, tn_in)
    meta_rows = jnp.zeros((CHUNK, D_MODEL), F32).at[CHUNK - N_META:].set(meta_tokens.astype(F32))
    p_meta, lora_meta = _in_proj(meta_rows, g_pre, w_main, w_lo, CHUNK, tn_in)

    mu = rwkv_mu[0]
    mu3 = mu[:3 * WIDTH].reshape(3, WIDTH)
    mu_lora = mu[3 * WIDTH:][None, :]
    as_row = lambda t: t[0].reshape(1, WIDTH)
    rows_vec = (mu3, as_row(rwkv_w0), as_row(rwkv_a0), as_row(rwkv_k_k), as_row(rwkv_k_a),
                as_row(rwkv_r_k), as_row(rwkv_gn_g), as_row(rwkv_gn_b))
    w_lora_up = jnp.concatenate([rwkv_w_up[0], rwkv_a_up[0]], axis=0).astype(BF16)
    idx = jnp.arange(256)
    e256 = (idx[:, None] // HEAD_DIM == idx[None, :] // HEAD_DIM).astype(BF16)
    zero_start = (jnp.zeros((PAIRS, LANES, LANES), F32), jnp.zeros((8, WIDTH), F32),
                  jnp.zeros((8, 2 * LORA), F32))
    _, *after_meta = _rwkv(p_meta, lora_meta, zero_start, rows_vec, mu_lora, w_lora_up, e256,
                           1, CHUNK, 1)
    y_rwkv, *_ = _rwkv(p_main, lora_main, after_meta, rows_vec, mu_lora, w_lora_up, e256,
                       batch, seq, RW_CHUNKS)

    kidx = jnp.arange(SUB)
    u_tri = (kidx[:, None] > kidx[None, :]).astype(BF16)
    pad = jnp.zeros((LANES - N_META, WIDTH), BF16)
    meta_seg = lambda seg: jnp.concatenate(
        [p_meta[CHUNK - N_META:, seg * WIDTH:(seg + 1) * WIDTH], pad], axis=0)
    y_sb = _sb_attn(p_main, meta_seg(SEG_KS), meta_seg(SEG_VS), u_tri, batch, seq, tq)

    out = _out_proj(y_rwkv, y_sb, p_main, x2d, w_proj_rwkv[0].astype(BF16),
                    w_proj_sb[0].astype(BF16), w_out[0].astype(BF16), post_norm_g[0][None, :],
                    tm_out)
    return out.reshape(batch, seq, D_MODEL)
```

```python
import functools

import jax
import jax.numpy as jnp
from jax import lax
from jax.experimental import pallas as pl
from jax.experimental.pallas import tpu as pltpu

F32 = jnp.float32
BF16 = jnp.bfloat16

D_MODEL = 1024
N_META = 16
HEADS = 16
HEAD_DIM = 64
WIDTH = HEADS * HEAD_DIM
LORA = 64
RMS_EPS = 1e-6
GN_EPS = 64e-5
L2_EPS = 1e-12
DECAY_SCALE = 0.6065306597126334
LOG2E = 1.4426950408889634

LANES = 128
PAIRS = WIDTH // LANES
CHUNK = 64
RW_CHUNKS = 8
MXU_DIM = 256
SUB = MXU_DIM
DEAD_CARRY = 110.0
N_SEG = 10
SEG_R, SEG_K, SEG_V, SEG_GR, SEG_Q, SEG_KS, SEG_VS, SEG_GS, SEG_MR, SEG_MS = range(N_SEG)
VMEM_LIMIT = 56 * 1024 * 1024


def _dot(a, b):
    return jnp.dot(a.astype(BF16), b.astype(BF16), preferred_element_type=F32)


def _dot_nt(a, b):
    return lax.dot_general(a.astype(BF16), b.astype(BF16), (((1,), (1,)), ((), ())),
                           preferred_element_type=F32)


def _sigmoid(x):
    return 1.0 / (1.0 + jnp.exp(-x))


def _in_proj_kernel(x_ref, g_ref, w_ref, wl_ref, p_ref, lora_ref, hn_ref):
    @pl.when(pl.program_id(1) == 0)
    def _():
        x = x_ref[...]
        inv = lax.rsqrt(jnp.mean(x * x, axis=-1, keepdims=True) + RMS_EPS)
        hn = (x * inv * g_ref[...]).astype(BF16)
        hn_ref[...] = hn
        lora_ref[...] = jnp.dot(hn, wl_ref[...], preferred_element_type=F32)

    p_ref[...] = jnp.dot(hn_ref[...], w_ref[...], preferred_element_type=F32).astype(p_ref.dtype)


def _in_proj(x2d, g, w_main, w_lora, tm, tn):
    rows = x2d.shape[0]
    cols = w_main.shape[1]
    return pl.pallas_call(
        _in_proj_kernel,
        grid=(rows // tm, cols // tn),
        in_specs=[
            pl.BlockSpec((tm, D_MODEL), lambda i, j: (i, 0)),
            pl.BlockSpec((1, D_MODEL), lambda i, j: (0, 0)),
            pl.BlockSpec((D_MODEL, tn), lambda i, j: (0, j)),
            pl.BlockSpec((D_MODEL, 2 * LORA), lambda i, j: (0, 0)),
        ],
        out_specs=[
            pl.BlockSpec((tm, tn), lambda i, j: (i, j)),
            pl.BlockSpec((tm, 2 * LORA), lambda i, j: (i, 0)),
        ],
        out_shape=[
            jax.ShapeDtypeStruct((rows, cols), BF16),
            jax.ShapeDtypeStruct((rows, 2 * LORA), F32),
        ],
        scratch_shapes=[pltpu.VMEM((tm, D_MODEL), BF16)],
        compiler_params=pltpu.CompilerParams(
            dimension_semantics=("arbitrary", "arbitrary"), vmem_limit_bytes=VMEM_LIMIT),
        name="in_proj",
    )(x2d, g, w_main, w_lora)


def _head_sum(x, e256):
    rows = x.shape[0]
    groups = WIDTH // MXU_DIM
    stacked = jnp.concatenate([x[:, MXU_DIM * g:MXU_DIM * (g + 1)] for g in range(groups)], axis=0)
    s = _dot(stacked, e256)
    return jnp.concatenate([s[rows * g:rows * (g + 1), :] for g in range(groups)], axis=1)


def _rwkv_kernel(r_ref, k_ref, v_ref, gr_ref, lo_ref, s0_ref, last0_ref, lastl0_ref,
                 mu_ref, mul_ref, w0_ref, a0_ref, kk_ref, ka_ref, rk_ref, gng_ref, gnb_ref,
                 wl_ref, e256_ref, o_ref, s_out, last_out, lastl_out, s_ref, last_ref, lastl_ref,
                 *, n_chunks):
    n_tok = n_chunks * CHUNK

    @pl.when(pl.program_id(1) == 0)
    def _():
        s_ref[...] = s0_ref[...]
        last_ref[...] = last0_ref[...]
        lastl_ref[...] = lastl0_ref[...]

    tok = lax.broadcasted_iota(jnp.int32, (n_tok, 1), 0)

    def shift(cur, last_row, mu):
        prev = jnp.where(tok == 0, last_row, pltpu.roll(cur, 1, 0))
        return cur + (prev - cur) * mu

    r_raw = r_ref[...].astype(F32)
    k_raw = k_ref[...].astype(F32)
    v_raw = v_ref[...].astype(F32)
    lo_raw = lo_ref[...]
    r = shift(r_raw, last_ref[0:1, :], mu_ref[0:1, :])
    k = shift(k_raw, last_ref[1:2, :], mu_ref[1:2, :])
    v = shift(v_raw, last_ref[2:3, :], mu_ref[2:3, :])
    lo = shift(lo_raw, lastl_ref[0:1, :], mul_ref[...])
    last_ref[0:1, :] = r_raw[n_tok - 1:n_tok, :]
    last_ref[1:2, :] = k_raw[n_tok - 1:n_tok, :]
    last_ref[2:3, :] = v_raw[n_tok - 1:n_tok, :]
    lastl_ref[0:1, :] = lo_raw[n_tok - 1:n_tok, :]

    chunk_shift = CHUNK.bit_length() - 1
    ti = lax.broadcasted_iota(jnp.int32, (CHUNK, CHUNK), 0)
    si = lax.broadcasted_iota(jnp.int32, (CHUNK, CHUNK), 1)
    tri = (si <= ti).astype(BF16)
    lane_l = lax.broadcasted_iota(jnp.int32, (CHUNK, 2 * LORA), 1)
    e256 = e256_ref[...]
    w_lora = wl_ref[...]

    row = lax.broadcasted_iota(jnp.int32, (CHUNK, 1), 0)
    lane = lax.broadcasted_iota(jnp.int32, (CHUNK, LANES), 1)
    first = lane < HEAD_DIM
    col = jnp.where(first, lane, lane - HEAD_DIM)
    strict = col < row
    incl = col <= row
    row2 = lax.broadcasted_iota(jnp.int32, (2 * CHUNK, LANES), 0)
    lane2 = lax.broadcasted_iota(jnp.int32, (2 * CHUNK, LANES), 1)
    same_head = (row2 < HEAD_DIM) == (lane2 < HEAD_DIM)
    eye2 = (row2 == lane2).astype(F32)
    pairs = range(PAIRS)

    def blockdiag(t):
        return jnp.concatenate([jnp.where(first, t, 0.0), jnp.where(first, 0.0, t)], axis=0)

    def pair(x, p):
        return x[:, LANES * p:LANES * (p + 1)]

    def prepare(ch, a):
        rows = slice(CHUNK * ch, CHUNK * (ch + 1))
        r_c, k_c, v_c, lo_c = r[rows], k[rows], v[rows], lo[rows]
        pre_w = _dot(jnp.where(lane_l < LORA, jnp.tanh(lo_c), 0.0), w_lora)
        pre_a = _dot(jnp.where(lane_l >= LORA, lo_c, 0.0), w_lora)
        kk = k_c * kk_ref[...]
        kk_sq = _head_sum(kk * kk, e256)
        yield
        logw = -DECAY_SCALE * _sigmoid(w0_ref[...] + pre_w)
        a_lr = _sigmoid(a0_ref[...] + pre_a)
        lw_hi = logw.astype(BF16)
        lw_lo = (logw - lw_hi.astype(F32)).astype(BF16)
        cw = jnp.dot(jnp.concatenate([tri, tri], axis=1), jnp.concatenate([lw_hi, lw_lo], axis=0),
                     preferred_element_type=F32)
        yield
        w_in = jnp.exp(cw)
        w_ex = jnp.exp(cw - logw)
        w_inv = jnp.exp(-cw)
        kappa = kk * lax.rsqrt(kk_sq + L2_EPS)
        k_rep = k_c * (1.0 + (a_lr - 1.0) * ka_ref[...])
        kappa_h = kappa * w_ex
        r_h = r_c * w_in
        k_h = k_rep * w_inv
        b_h = kappa * a_lr * w_inv
        a.update(
            r=r_c, k_rep=k_rep, v=v_c, w_end=w_in[CHUNK - 1:CHUNK, :],
            lhs=[jnp.concatenate([pair(kappa_h, p), pair(r_h, p)], axis=0).astype(BF16)
                 for p in pairs],
            kb=[jnp.concatenate([pair(k_h, p), pair(b_h, p)], axis=0).astype(BF16) for p in pairs],
            kbd=[jnp.concatenate([blockdiag(pair(k_h, p)), blockdiag(pair(b_h, p))],
                                 axis=0).astype(BF16) for p in pairs],
            vbd=[blockdiag(pair(v_c, p)).astype(BF16) for p in pairs])

    def independent(a):
        gram = [_dot_nt(a["lhs"][p], a["kbd"][p]) for p in pairs]
        yield
        l_bd = [blockdiag(jnp.where(strict, g[:CHUNK, LANES:], 0.0)) for g in gram]
        mk = [jnp.concatenate([jnp.where(strict, g[:CHUNK, :LANES], 0.0),
                               jnp.where(incl, g[CHUNK:, :LANES], 0.0)], axis=0) for g in gram]
        a["mb_r"] = [jnp.where(incl, g[CHUNK:, LANES:], 0.0) for g in gram]
        a["mv"] = [_dot(mk[p], a["vbd"][p]) for p in pairs]
        yield
        t_inv = None
        for level in range(chunk_shift):
            size = 1 << level
            rb = jnp.right_shift(row2, level)
            off = ((rb & 1) == 1) & (jnp.right_shift(lane2, level) == rb - 1)
            l_off = [jnp.where(off, l, 0.0) for l in l_bd]
            if level == 0:
                t_inv = [eye2 - l for l in l_off]
            elif size < 8:
                lt = [_dot(l_off[p], t_inv[p]) for p in pairs]
                yield
                t_inv = [t_inv[p] - _dot(t_inv[p], lt[p]) for p in pairs]
                yield
            else:
                odd = [slice(start, start + size) for start in range(size, 2 * CHUNK, 2 * size)]

                def take(x):
                    return jnp.concatenate([x[rows] for rows in odd], axis=0)

                def spread(x_odd, x_even):
                    pieces = []
                    for n, rows in enumerate(odd):
                        pieces += [x_even[rows.start - size:rows.start], x_odd[size * n:size * (n + 1)]]
                    return jnp.concatenate(pieces, axis=0)

                lt_odd = [_dot(take(l_off[p]), t_inv[p]) for p in pairs]
                yield
                t_odd = [take(t_inv[p]) for p in pairs]
                zero = jnp.zeros((2 * CHUNK, LANES), F32)
                delta = [_dot(t_odd[p], spread(lt_odd[p], zero)) for p in pairs]
                yield
                t_inv = [spread(t_odd[p] - delta[p], t_inv[p]) for p in pairs]
        a["t_inv"] = t_inv

    state = [s_ref[p] for p in pairs]

    def dependent(a):
        sp = [_dot_nt(a["lhs"][p], state[p]) for p in pairs]
        yield
        u_bd = [_dot(a["t_inv"][p], blockdiag(sp[p][:CHUNK] + a["mv"][p][:CHUNK])) for p in pairs]
        yield
        vu_t = [jnp.concatenate([pair(a["v"], p), -(u_bd[p][:CHUNK] + u_bd[p][CHUNK:])], axis=0).T
                for p in pairs]
        ds = [_dot(vu_t[p], a["kb"][p]) for p in pairs]
        yd = [_dot(a["mb_r"][p], u_bd[p]) for p in pairs]
        yield
        for p in pairs:
            state[p] = (state[p] + jnp.where(same_head, ds[p], 0.0)) * pair(a["w_end"], p)
        a["y"] = jnp.concatenate(
            [sp[p][CHUNK:] + a["mv"][p][CHUNK:] - yd[p] for p in pairs], axis=1)

    def finish(ch, a):
        y = a["y"]
        mean = _head_sum(y, e256) * (1.0 / HEAD_DIM)
        yield
        d = y - mean
        var = _head_sum(d * d, e256) * (1.0 / HEAD_DIM)
        yield
        y = d * lax.rsqrt(var + GN_EPS) * gng_ref[...] + gnb_ref[...]
        bonus = _head_sum(a["r"] * a["k_rep"] * rk_ref[...], e256) * a["v"]
        yield
        rows = slice(CHUNK * ch, CHUNK * (ch + 1))
        g = gr_ref[rows, :].astype(F32)
        o_ref[rows, :] = ((y + bonus) * (g * _sigmoid(g))).astype(o_ref.dtype)

    chunks = [dict() for _ in range(n_chunks)]
    for ch in range(min(2, n_chunks)):
        for _ in prepare(ch, chunks[ch]):
            pass
    to_prepare = list(range(2, n_chunks))
    waiting = list(range(n_chunks))
    prep, running, finished, dep, dep_ch, tails = None, [], set(), None, 0, []
    while waiting or running or dep is not None or dep_ch < n_chunks or tails:
        if prep is None and to_prepare:
            prep = prepare(to_prepare[0], chunks[to_prepare[0]])
        while waiting and len(running) < 2 and "lhs" in chunks[waiting[0]]:
            ch = waiting.pop(0)
            running.append((ch, independent(chunks[ch])))
        if dep is None and dep_ch in finished:
            dep = dependent(chunks[dep_ch])
        for item in list(running):
            if next(item[1], "done") == "done":
                running.remove(item)
                finished.add(item[0])
        if prep is not None and next(prep, "done") == "done":
            prep = None
            to_prepare.pop(0)
        if dep is not None and next(dep, "done") == "done":
            tails.append(finish(dep_ch, chunks[dep_ch]))
            dep, dep_ch = None, dep_ch + 1
        for gen in list(tails):
            if next(gen, "done") == "done":
                tails.remove(gen)
    for p in pairs:
        s_ref[p] = state[p]
        s_out[p] = state[p]
    last_out[...] = last_ref[...]
    lastl_out[...] = lastl_ref[...]


def _rwkv(p, lora, start, rows_vec, mu_lora, w_lora, e256, batch, seq, n_chunks):
    n_tok = n_chunks * CHUNK
    per_b = seq // n_tok

    def block(seg):
        return lambda b, c: (b * per_b + c, seg)

    def whole(shape):
        return pl.BlockSpec(shape, lambda b, c: (0,) * len(shape))

    state_shapes = [(PAIRS, LANES, LANES), (8, WIDTH), (8, 2 * LORA)]
    in_specs = [
        pl.BlockSpec((n_tok, WIDTH), block(SEG_R)),
        pl.BlockSpec((n_tok, WIDTH), block(SEG_K)),
        pl.BlockSpec((n_tok, WIDTH), block(SEG_V)),
        pl.BlockSpec((n_tok, WIDTH), block(SEG_GR)),
        pl.BlockSpec((n_tok, 2 * LORA), block(0)),
    ] + [whole(s) for s in state_shapes] + [whole((3, WIDTH)), whole((1, 2 * LORA))
    ] + [whole((1, WIDTH))] * 7 + [whole((2 * LORA, WIDTH)), whole((MXU_DIM, MXU_DIM))]
    mu3, w0, a0, k_k, k_a, r_k, gn_g, gn_b = rows_vec
    return pl.pallas_call(
        functools.partial(_rwkv_kernel, n_chunks=n_chunks),
        grid=(batch, per_b),
        in_specs=in_specs,
        out_specs=[pl.BlockSpec((n_tok, WIDTH), block(0))] + [whole(s) for s in state_shapes],
        out_shape=[jax.ShapeDtypeStruct((batch * seq, WIDTH), BF16)]
        + [jax.ShapeDtypeStruct(s, F32) for s in state_shapes],
        scratch_shapes=[pltpu.VMEM(s, F32) for s in state_shapes],
        compiler_params=pltpu.CompilerParams(
            dimension_semantics=("arbitrary", "arbitrary"), vmem_limit_bytes=VMEM_LIMIT),
        name="rwkv",
    )(p, p, p, p, lora, *start, mu3, mu_lora, w0, a0, k_k, k_a, r_k, gn_g, gn_b, w_lora, e256)


def _sb_kernel(q_ref, k_ref, v_ref, km_ref, vm_ref, g_ref, u_ref, o_ref, *, tq):
    i = pl.program_id(2)
    n_sub = tq // SUB
    u = u_ref[...]
    first = lax.broadcasted_iota(jnp.int32, (SUB, LANES), 1) < HEAD_DIM
    row = lax.broadcasted_iota(jnp.int32, (2 * SUB, SUB), 0)
    lane_k = lax.broadcasted_iota(jnp.int32, (2 * SUB, SUB), 1)
    causal = lane_k < jnp.where(row < SUB, row, row - SUB)
    meta_visible = lax.broadcasted_iota(jnp.int32, (2 * SUB, LANES), 1) < N_META

    def stacked_q(s):
        q = q_ref[SUB * s:SUB * (s + 1), :].astype(F32) * (HEAD_DIM ** -0.5)
        return jnp.concatenate([jnp.where(first, q, 0.0), jnp.where(first, 0.0, q)],
                               axis=0).astype(BF16)

    def cores(items):
        zs = [_dot_nt(qs, kb) for qs, kb, _, _ in items]
        sps = []
        for z, (_, _, _, visible) in zip(zs, items):
            sp = jnp.maximum(z, 0.0) + jnp.log(1.0 + jnp.exp2(jnp.abs(z) * (-LOG2E)))
            sps.append(sp if visible is None else jnp.where(visible, sp, 0.0))
        css = [jnp.dot(sp.astype(BF16), u[:sp.shape[1], :sp.shape[1]], preferred_element_type=F32)
               for sp in sps]
        out = []
        for z, sp, cs, (_, _, vb, visible) in zip(zs, sps, css, items):
            w = jnp.exp(z - sp - cs)
            if visible is not None:
                w = jnp.where(visible, w, 0.0)
            wcat = jnp.concatenate([w[:SUB], w[SUB:]], axis=1).astype(BF16)
            first_v = lax.broadcasted_iota(jnp.int32, vb.shape, 1) < HEAD_DIM
            vbd = jnp.concatenate([jnp.where(first_v, vb, 0), jnp.where(first_v, 0, vb)], axis=0)
            out.append((jnp.dot(wcat, vbd, preferred_element_type=F32), cs[:, 0:1] + sp[:, 0:1]))
        return out

    def nearer(carry):
        f = jnp.exp(-carry)
        return jnp.where(first, f[:SUB], f[SUB:])

    def keys(j):
        start = pl.multiple_of(j * SUB, SUB)
        return k_ref[pl.ds(start, SUB), :], v_ref[pl.ds(start, SUB), :]

    qss = [stacked_q(s) for s in range(n_sub)]
    items = []
    for s in range(n_sub):
        ii = i * n_sub + s
        items.append((qss[s],) + keys(ii) + (causal,))
        items.append((qss[s],) + keys(jnp.maximum(ii - 1, 0)) + (None,))
    first_two = cores(items)

    carries, accs = [], []
    for s in range(n_sub):
        (pv_d, tot_d), (pv_p, tot_p) = first_two[2 * s], first_two[2 * s + 1]
        has_prev = (i * n_sub + s > 0).astype(F32)
        accs.append(pv_d + pv_p * (nearer(tot_d) * has_prev))
        carries.append(tot_d + tot_p * has_prev)

    def alive(carry):
        return jnp.min(carry) <= DEAD_CARRY

    def farther(*operands):
        out = []
        for s in range(n_sub):
            carry, acc = operands[s], operands[n_sub + s]

            def body(state, s=s):
                j, _, carry, acc = state
                (pv, tot), = cores([(qss[s],) + keys(j) + (None,)])
                carry_new = carry + tot
                return j - 1, alive(carry_new), carry_new, acc + pv * nearer(carry)

            _, still, carry, acc = lax.while_loop(
                lambda state: (state[0] >= 0) & state[1], body,
                (i * n_sub + s - 2, alive(carry), carry, acc))

            def with_meta(carry, acc, s=s):
                (pv, _), = cores([(qss[s], km_ref[...], vm_ref[...], meta_visible)])
                return acc + pv * nearer(carry)

            out.append(lax.cond(still, with_meta, lambda carry, acc: acc, carry, acc))
        return tuple(out)

    any_alive = functools.reduce(jnp.logical_or, [alive(carry) for carry in carries])
    accs = lax.cond(any_alive, farther, lambda *operands: tuple(operands[n_sub:]), *carries, *accs)
    for s in range(n_sub):
        g = g_ref[SUB * s:SUB * (s + 1), :].astype(F32)
        o_ref[SUB * s:SUB * (s + 1), :] = (accs[s] * (g * _sigmoid(g))).astype(o_ref.dtype)


def _sb_attn(p_main, k_meta, v_meta, u, batch, seq, tq):
    nq = seq // tq

    def seq_map(seg):
        return lambda b, p, i: (b, seg * PAIRS + p)

    def tile_map(seg):
        return lambda b, p, i: (b * nq + i, seg * PAIRS + p)

    return pl.pallas_call(
        functools.partial(_sb_kernel, tq=tq),
        grid=(batch, PAIRS, nq),
        in_specs=[
            pl.BlockSpec((tq, LANES), tile_map(SEG_Q)),
            pl.BlockSpec((seq, LANES), seq_map(SEG_KS)),
            pl.BlockSpec((seq, LANES), seq_map(SEG_VS)),
            pl.BlockSpec((LANES, LANES), lambda b, p, i: (0, p)),
            pl.BlockSpec((LANES, LANES), lambda b, p, i: (0, p)),
            pl.BlockSpec((tq, LANES), tile_map(SEG_GS)),
            pl.BlockSpec((SUB, SUB), lambda b, p, i: (0, 0)),
        ],
        out_specs=pl.BlockSpec((tq, LANES), lambda b, p, i: (b * nq + i, p)),
        out_shape=jax.ShapeDtypeStruct((batch * seq, WIDTH), BF16),
        compiler_params=pltpu.CompilerParams(
            dimension_semantics=("arbitrary", "arbitrary", "arbitrary"),
            vmem_limit_bytes=VMEM_LIMIT),
        name="sb_attn",
    )(p_main, p_main, p_main, k_meta, v_meta, p_main, u)


def _out_kernel(yr_ref, ys_ref, mr_ref, ms_ref, x_ref, wr_ref, ws_ref, wo_ref, g_ref, o_ref):
    pr = jnp.dot(yr_ref[...], wr_ref[...], preferred_element_type=F32)
    ps = jnp.dot(ys_ref[...], ws_ref[...], preferred_element_type=F32)
    mixed = (_sigmoid(mr_ref[...].astype(F32)) * pr + _sigmoid(ms_ref[...].astype(F32)) * ps)
    o = jnp.dot(mixed.astype(BF16), wo_ref[...], preferred_element_type=F32)
    inv = lax.rsqrt(jnp.mean(o * o, axis=-1, keepdims=True) + RMS_EPS)
    o_ref[...] = x_ref[...] + o * inv * g_ref[...]


def _out_proj(y_rwkv, y_sb, p_main, x2d, w_r, w_s, w_o, g, tm):
    rows = x2d.shape[0]
    w_spec = pl.BlockSpec((WIDTH, D_MODEL), lambda i: (0, 0))
    return pl.pallas_call(
        _out_kernel,
        grid=(rows // tm,),
        in_specs=[
            pl.BlockSpec((tm, WIDTH), lambda i: (i, 0)),
            pl.BlockSpec((tm, WIDTH), lambda i: (i, 0)),
            pl.BlockSpec((tm, WIDTH), lambda i: (i, SEG_MR)),
            pl.BlockSpec((tm, WIDTH), lambda i: (i, SEG_MS)),
            pl.BlockSpec((tm, D_MODEL), lambda i: (i, 0)),
            w_spec, w_spec, w_spec,
            pl.BlockSpec((1, D_MODEL), lambda i: (0, 0)),
        ],
        out_specs=pl.BlockSpec((tm, D_MODEL), lambda i: (i, 0)),
        out_shape=jax.ShapeDtypeStruct((rows, D_MODEL), F32),
        compiler_params=pltpu.CompilerParams(
            dimension_semantics=("arbitrary",), vmem_limit_bytes=VMEM_LIMIT),
        name="out_proj",
    )(y_rwkv, y_sb, p_main, p_main, x2d, w_r, w_s, w_o, g)


def _tiles(batch, seq):
    rows = batch * seq
    tm_in = min(1024, rows)
    tm_out = min(512, rows)
    tq = min(4096, seq)
    return tm_in, 2560, tm_out, tq


def kernel(x, meta_tokens, pre_norm_g, post_norm_g, w_in, rwkv_mu, rwkv_w0, rwkv_w_up, rwkv_a0,
           rwkv_a_up, rwkv_k_k, rwkv_k_a, rwkv_r_k, rwkv_gn_g, rwkv_gn_b, w_proj_rwkv, w_proj_sb,
           w_out):
    batch, seq, d_model = x.shape
    assert d_model == D_MODEL and w_in.shape[0] == 1, "single-layer kernel"
    assert seq % 512 == 0 and meta_tokens.shape == (N_META, D_MODEL)
    tm_in, tn_in, tm_out, tq = _tiles(batch, seq)

    w = w_in[0].astype(BF16)
    w_main = jnp.concatenate([w[:, :3 * WIDTH], w[:, 3 * WIDTH + 2 * LORA:]], axis=1)
    w_lo = w[:, 3 * WIDTH:3 * WIDTH + 2 * LORA]
    g_pre = pre_norm_g[0][None, :]

    x2d = x.reshape(batch * seq, D_MODEL)
    p_main, lora_main = _in_proj(x2d, g_pre, w_main, w_lo, tm_in, tn_in)
    meta_rows = jnp.zeros((CHUNK, D_MODEL), F32).at[CHUNK - N_META:].set(meta_tokens.astype(F32))
    p_meta, lora_meta = _in_proj(meta_rows, g_pre, w_main, w_lo, CHUNK, tn_in)

    mu = rwkv_mu[0]
    mu3 = mu[:3 * WIDTH].reshape(3, WIDTH)
    mu_lora = mu[3 * WIDTH:][None, :]
    as_row = lambda t: t[0].reshape(1, WIDTH)
    rows_vec = (mu3, as_row(rwkv_w0), as_row(rwkv_a0), as_row(rwkv_k_k), as_row(rwkv_k_a),
                as_row(rwkv_r_k), as_row(rwkv_gn_g), as_row(rwkv_gn_b))
    w_lora_up = jnp.concatenate([rwkv_w_up[0], rwkv_a_up[0]], axis=0).astype(BF16)
    idx = jnp.arange(MXU_DIM)
    e256 = (idx[:, None] // HEAD_DIM == idx[None, :] // HEAD_DIM).astype(BF16)
    zero_start = (jnp.zeros((PAIRS, LANES, LANES), F32), jnp.zeros((8, WIDTH), F32),
                  jnp.zeros((8, 2 * LORA), F32))
    _, *after_meta = _rwkv(p_meta, lora_meta, zero_start, rows_vec, mu_lora, w_lora_up, e256,
                           1, CHUNK, 1)
    y_rwkv, *_ = _rwkv(p_main, lora_main, after_meta, rows_vec, mu_lora, w_lora_up, e256,
                       batch, seq, RW_CHUNKS)

    kidx = jnp.arange(SUB)
    u_tri = (kidx[:, None] > kidx[None, :]).astype(BF16)
    pad = jnp.zeros((LANES - N_META, WIDTH), BF16)
    meta_seg = lambda seg: jnp.concatenate(
        [p_meta[CHUNK - N_META:, seg * WIDTH:(seg + 1) * WIDTH], pad], axis=0)
    y_sb = _sb_attn(p_main, meta_seg(SEG_KS), meta_seg(SEG_VS), u_tri, batch, seq, tq)

    out = _out_proj(y_rwkv, y_sb, p_main, x2d, w_proj_rwkv[0].astype(BF16),
                    w_proj_sb[0].astype(BF16), w_out[0].astype(BF16), post_norm_g[0][None, :],
                    tm_out)
    return out.reshape(batch, seq, D_MODEL)
```

```python
import functools

import jax
import jax.numpy as jnp
from jax import lax
from jax.experimental import pallas as pl
from jax.experimental.pallas import tpu as pltpu

F32 = jnp.float32
BF16 = jnp.bfloat16

D_MODEL = 1024
N_META = 16
HEADS = 16
HEAD_DIM = 64
WIDTH = HEADS * HEAD_DIM
LORA = 64
RMS_EPS = 1e-6
GN_EPS = 64e-5
L2_EPS = 1e-12
DECAY_SCALE = 0.6065306597126334
LOG2E = 1.4426950408889634

LANES = 128
PAIRS = WIDTH // LANES
CHUNK = 64
RW_CHUNKS = 8
SUB = 256
DEAD_CARRY = 110.0
N_SEG = 10
SEG_R, SEG_K, SEG_V, SEG_GR, SEG_Q, SEG_KS, SEG_VS, SEG_GS, SEG_MR, SEG_MS = range(N_SEG)
VMEM_LIMIT = 56 * 1024 * 1024


def _dot(a, b):
    return jnp.dot(a.astype(BF16), b.astype(BF16), preferred_element_type=F32)


def _dot_nt(a, b):
    return lax.dot_general(a.astype(BF16), b.astype(BF16), (((1,), (1,)), ((), ())),
                           preferred_element_type=F32)


def _sigmoid(x):
    return 1.0 / (1.0 + jnp.exp(-x))


def _in_proj_kernel(x_ref, g_ref, w_ref, wl_ref, p_ref, lora_ref, hn_ref):
    @pl.when(pl.program_id(1) == 0)
    def _():
        x = x_ref[...]
        inv = lax.rsqrt(jnp.mean(x * x, axis=-1, keepdims=True) + RMS_EPS)
        hn = (x * inv * g_ref[...]).astype(BF16)
        hn_ref[...] = hn
        lora_ref[...] = jnp.dot(hn, wl_ref[...], preferred_element_type=F32)

    p_ref[...] = jnp.dot(hn_ref[...], w_ref[...], preferred_element_type=F32).astype(p_ref.dtype)


def _in_proj(x2d, g, w_main, w_lora, tm, tn):
    rows = x2d.shape[0]
    cols = w_main.shape[1]
    return pl.pallas_call(
        _in_proj_kernel,
        grid=(rows // tm, cols // tn),
        in_specs=[
            pl.BlockSpec((tm, D_MODEL), lambda i, j: (i, 0)),
            pl.BlockSpec((1, D_MODEL), lambda i, j: (0, 0)),
            pl.BlockSpec((D_MODEL, tn), lambda i, j: (0, j)),
            pl.BlockSpec((D_MODEL, 2 * LORA), lambda i, j: (0, 0)),
        ],
        out_specs=[
            pl.BlockSpec((tm, tn), lambda i, j: (i, j)),
            pl.BlockSpec((tm, 2 * LORA), lambda i, j: (i, 0)),
        ],
        out_shape=[
            jax.ShapeDtypeStruct((rows, cols), BF16),
            jax.ShapeDtypeStruct((rows, 2 * LORA), F32),
        ],
        scratch_shapes=[pltpu.VMEM((tm, D_MODEL), BF16)],
        compiler_params=pltpu.CompilerParams(
            dimension_semantics=("arbitrary", "arbitrary"), vmem_limit_bytes=VMEM_LIMIT),
        name="in_proj",
    )(x2d, g, w_main, w_lora)


def _head_sum(x, e256):
    rows = x.shape[0]
    groups = WIDTH // 256
    stacked = jnp.concatenate([x[:, 256 * g:256 * (g + 1)] for g in range(groups)], axis=0)
    s = _dot(stacked, e256)
    return jnp.concatenate([s[rows * g:rows * (g + 1), :] for g in range(groups)], axis=1)


def _rwkv_kernel(r_ref, k_ref, v_ref, gr_ref, lo_ref, s0_ref, last0_ref, lastl0_ref,
                 mu_ref, mul_ref, w0_ref, a0_ref, kk_ref, ka_ref, rk_ref, gng_ref, gnb_ref,
                 wl_ref, e256_ref, o_ref, s_out, last_out, lastl_out, s_ref, last_ref, lastl_ref,
                 *, n_chunks):
    n_tok = n_chunks * CHUNK

    @pl.when(pl.program_id(1) == 0)
    def _():
        s_ref[...] = s0_ref[...]
        last_ref[...] = last0_ref[...]
        lastl_ref[...] = lastl0_ref[...]

    tok = lax.broadcasted_iota(jnp.int32, (n_tok, 1), 0)

    def shift(cur, last_row, mu):
        prev = jnp.where(tok == 0, last_row, pltpu.roll(cur, 1, 0))
        return cur + (prev - cur) * mu

    r_raw = r_ref[...].astype(F32)
    k_raw = k_ref[...].astype(F32)
    v_raw = v_ref[...].astype(F32)
    lo_raw = lo_ref[...]
    r = shift(r_raw, last_ref[0:1, :], mu_ref[0:1, :])
    k = shift(k_raw, last_ref[1:2, :], mu_ref[1:2, :])
    v = shift(v_raw, last_ref[2:3, :], mu_ref[2:3, :])
    lo = shift(lo_raw, lastl_ref[0:1, :], mul_ref[...])
    last_ref[0:1, :] = r_raw[n_tok - 1:n_tok, :]
    last_ref[1:2, :] = k_raw[n_tok - 1:n_tok, :]
    last_ref[2:3, :] = v_raw[n_tok - 1:n_tok, :]
    lastl_ref[0:1, :] = lo_raw[n_tok - 1:n_tok, :]

    chunk_shift = CHUNK.bit_length() - 1
    ti = lax.broadcasted_iota(jnp.int32, (CHUNK, CHUNK), 0)
    si = lax.broadcasted_iota(jnp.int32, (CHUNK, CHUNK), 1)
    tri = (si <= ti).astype(BF16)
    lane_l = lax.broadcasted_iota(jnp.int32, (CHUNK, 2 * LORA), 1)
    e256 = e256_ref[...]
    w_lora = wl_ref[...]

    row = lax.broadcasted_iota(jnp.int32, (CHUNK, 1), 0)
    lane = lax.broadcasted_iota(jnp.int32, (CHUNK, LANES), 1)
    first = lane < HEAD_DIM
    col = jnp.where(first, lane, lane - HEAD_DIM)
    strict = col < row
    incl = col <= row
    row2 = lax.broadcasted_iota(jnp.int32, (2 * CHUNK, LANES), 0)
    lane2 = lax.broadcasted_iota(jnp.int32, (2 * CHUNK, LANES), 1)
    same_head = (row2 < HEAD_DIM) == (lane2 < HEAD_DIM)
    eye2 = (row2 == lane2).astype(F32)
    pairs = range(PAIRS)

    def blockdiag(t):
        return jnp.concatenate([jnp.where(first, t, 0.0), jnp.where(first, 0.0, t)], axis=0)

    def pair(x, p):
        return x[:, LANES * p:LANES * (p + 1)]

    def prepare(ch, a):
        rows = slice(CHUNK * ch, CHUNK * (ch + 1))
        r_c, k_c, v_c, lo_c = r[rows], k[rows], v[rows], lo[rows]
        pre_w = _dot(jnp.where(lane_l < LORA, jnp.tanh(lo_c), 0.0), w_lora)
        pre_a = _dot(jnp.where(lane_l >= LORA, lo_c, 0.0), w_lora)
        kk = k_c * kk_ref[...]
        kk_sq = _head_sum(kk * kk, e256)
        yield
        logw = -DECAY_SCALE * _sigmoid(w0_ref[...] + pre_w)
        a_lr = _sigmoid(a0_ref[...] + pre_a)
        lw_hi = logw.astype(BF16)
        lw_lo = (logw - lw_hi.astype(F32)).astype(BF16)
        cw = jnp.dot(jnp.concatenate([tri, tri], axis=1), jnp.concatenate([lw_hi, lw_lo], axis=0),
                     preferred_element_type=F32)
        yield
        w_in = jnp.exp(cw)
        w_ex = jnp.exp(cw - logw)
        w_inv = jnp.exp(-cw)
        kappa = kk * lax.rsqrt(kk_sq + L2_EPS)
        k_rep = k_c * (1.0 + (a_lr - 1.0) * ka_ref[...])
        kappa_h = kappa * w_ex
        r_h = r_c * w_in
        k_h = k_rep * w_inv
        b_h = kappa * a_lr * w_inv
        a.update(
            r=r_c, k_rep=k_rep, v=v_c, w_end=w_in[CHUNK - 1:CHUNK, :],
            lhs=[jnp.concatenate([pair(kappa_h, p), pair(r_h, p)], axis=0).astype(BF16)
                 for p in pairs],
            kb=[jnp.concatenate([pair(k_h, p), pair(b_h, p)], axis=0).astype(BF16) for p in pairs],
            kbd=[jnp.concatenate([blockdiag(pair(k_h, p)), blockdiag(pair(b_h, p))],
                                 axis=0).astype(BF16) for p in pairs],
            vbd=[blockdiag(pair(v_c, p)).astype(BF16) for p in pairs])

    def independent(a):
        gram = [_dot_nt(a["lhs"][p], a["kbd"][p]) for p in pairs]
        yield
        l_bd = [blockdiag(jnp.where(strict, g[:CHUNK, LANES:], 0.0)) for g in gram]
        mk = [jnp.concatenate([jnp.where(strict, g[:CHUNK, :LANES], 0.0),
                               jnp.where(incl, g[CHUNK:, :LANES], 0.0)], axis=0) for g in gram]
        a["mb_r"] = [jnp.where(incl, g[CHUNK:, LANES:], 0.0) for g in gram]
        a["mv"] = [_dot(mk[p], a["vbd"][p]) for p in pairs]
        yield
        t_inv = None
        for level in range(chunk_shift):
            size = 1 << level
            rb = jnp.right_shift(row2, level)
            off = ((rb & 1) == 1) & (jnp.right_shift(lane2, level) == rb - 1)
            l_off = [jnp.where(off, l, 0.0) for l in l_bd]
            if level == 0:
                t_inv = [eye2 - l for l in l_off]
            elif size < 8:
                lt = [_dot(l_off[p], t_inv[p]) for p in pairs]
                yield
                t_inv = [t_inv[p] - _dot(t_inv[p], lt[p]) for p in pairs]
                yield
            else:
                odd = [slice(start, start + size) for start in range(size, 2 * CHUNK, 2 * size)]

                def take(x):
                    return jnp.concatenate([x[rows] for rows in odd], axis=0)

                def spread(x_odd, x_even):
                    pieces = []
                    for n, rows in enumerate(odd):
                        pieces += [x_even[rows.start - size:rows.start], x_odd[size * n:size * (n + 1)]]
                    return jnp.concatenate(pieces, axis=0)

                lt_odd = [_dot(take(l_off[p]), t_inv[p]) for p in pairs]
                yield
                t_odd = [take(t_inv[p]) for p in pairs]
                zero = jnp.zeros((2 * CHUNK, LANES), F32)
                delta = [_dot(t_odd[p], spread(lt_odd[p], zero)) for p in pairs]
                yield
                t_inv = [spread(t_odd[p] - delta[p], t_inv[p]) for p in pairs]
        a["t_inv"] = t_inv

    state = [s_ref[p] for p in pairs]

    def dependent(a):
        sp = [_dot_nt(a["lhs"][p], state[p]) for p in pairs]
        yield
        u_bd = [_dot(a["t_inv"][p], blockdiag(sp[p][:CHUNK] + a["mv"][p][:CHUNK])) for p in pairs]
        yield
        vu_t = [jnp.concatenate([pair(a["v"], p), -(u_bd[p][:CHUNK] + u_bd[p][CHUNK:])], axis=0).T
                for p in pairs]
        ds = [_dot(vu_t[p], a["kb"][p]) for p in pairs]
        yd = [_dot(a["mb_r"][p], u_bd[p]) for p in pairs]
        yield
        for p in pairs:
            state[p] = (state[p] + jnp.where(same_head, ds[p], 0.0)) * pair(a["w_end"], p)
        a["y"] = jnp.concatenate(
            [sp[p][CHUNK:] + a["mv"][p][CHUNK:] - yd[p] for p in pairs], axis=1)

    def finish(ch, a):
        y = a["y"]
        mean = _head_sum(y, e256) * (1.0 / HEAD_DIM)
        yield
        d = y - mean
        var = _head_sum(d * d, e256) * (1.0 / HEAD_DIM)
        yield
        y = d * lax.rsqrt(var + GN_EPS) * gng_ref[...] + gnb_ref[...]
        bonus = _head_sum(a["r"] * a["k_rep"] * rk_ref[...], e256) * a["v"]
        yield
        rows = slice(CHUNK * ch, CHUNK * (ch + 1))
        g = gr_ref[rows, :].astype(F32)
        o_ref[rows, :] = ((y + bonus) * (g * _sigmoid(g))).astype(o_ref.dtype)

    chunks = [dict() for _ in range(n_chunks)]
    for ch in range(min(2, n_chunks)):
        for _ in prepare(ch, chunks[ch]):
            pass
    to_prepare = list(range(2, n_chunks))
    waiting = list(range(n_chunks))
    prep, running, finished, dep, dep_ch, tails = None, [], set(), None, 0, []
    while waiting or running or dep is not None or dep_ch < n_chunks or tails:
        if prep is None and to_prepare:
            prep = prepare(to_prepare[0], chunks[to_prepare[0]])
        while waiting and len(running) < 2 and "lhs" in chunks[waiting[0]]:
            ch = waiting.pop(0)
            running.append((ch, independent(chunks[ch])))
        if dep is None and dep_ch in finished:
            dep = dependent(chunks[dep_ch])
        for item in list(running):
            if next(item[1], "done") == "done":
                running.remove(item)
                finished.add(item[0])
        if prep is not None and next(prep, "done") == "done":
            prep = None
            to_prepare.pop(0)
        if dep is not None and next(dep, "done") == "done":
            tails.append(finish(dep_ch, chunks[dep_ch]))
            dep, dep_ch = None, dep_ch + 1
        for gen in list(tails):
            if next(gen, "done") == "done":
                tails.remove(gen)
    for p in pairs:
        s_ref[p] = state[p]
        s_out[p] = state[p]
    last_out[...] = last_ref[...]
    lastl_out[...] = lastl_ref[...]


def _rwkv(p, lora, start, rows_vec, mu_lora, w_lora, e256, batch, seq, n_chunks):
    n_tok = n_chunks * CHUNK
    per_b = seq // n_tok

    def block(seg):
        return lambda b, c: (b * per_b + c, seg)

    def whole(shape):
        return pl.BlockSpec(shape, lambda b, c: (0,) * len(shape))

    state_shapes = [(PAIRS, LANES, LANES), (8, WIDTH), (8, 2 * LORA)]
    in_specs = [
        pl.BlockSpec((n_tok, WIDTH), block(SEG_R)),
        pl.BlockSpec((n_tok, WIDTH), block(SEG_K)),
        pl.BlockSpec((n_tok, WIDTH), block(SEG_V)),
        pl.BlockSpec((n_tok, WIDTH), block(SEG_GR)),
        pl.BlockSpec((n_tok, 2 * LORA), block(0)),
    ] + [whole(s) for s in state_shapes] + [whole((3, WIDTH)), whole((1, 2 * LORA))
    ] + [whole((1, WIDTH))] * 7 + [whole((2 * LORA, WIDTH)), whole((256, 256))]
    mu3, w0, a0, k_k, k_a, r_k, gn_g, gn_b = rows_vec
    return pl.pallas_call(
        functools.partial(_rwkv_kernel, n_chunks=n_chunks),
        grid=(batch, per_b),
        in_specs=in_specs,
        out_specs=[pl.BlockSpec((n_tok, WIDTH), block(0))] + [whole(s) for s in state_shapes],
        out_shape=[jax.ShapeDtypeStruct((batch * seq, WIDTH), BF16)]
        + [jax.ShapeDtypeStruct(s, F32) for s in state_shapes],
        scratch_shapes=[pltpu.VMEM(s, F32) for s in state_shapes],
        compiler_params=pltpu.CompilerParams(
            dimension_semantics=("arbitrary", "arbitrary"), vmem_limit_bytes=VMEM_LIMIT),
        name="rwkv",
    )(p, p, p, p, lora, *start, mu3, mu_lora, w0, a0, k_k, k_a, r_k, gn_g, gn_b, w_lora, e256)


def _sb_kernel(q_ref, k_ref, v_ref, km_ref, vm_ref, g_ref, u_ref, o_ref, *, tq):
    i = pl.program_id(2)
    n_sub = tq // SUB
    u = u_ref[...]
    first = lax.broadcasted_iota(jnp.int32, (SUB, LANES), 1) < HEAD_DIM
    row = lax.broadcasted_iota(jnp.int32, (2 * SUB, SUB), 0)
    lane_k = lax.broadcasted_iota(jnp.int32, (2 * SUB, SUB), 1)
    causal = lane_k < jnp.where(row < SUB, row, row - SUB)
    meta_visible = lax.broadcasted_iota(jnp.int32, (2 * SUB, LANES), 1) < N_META

    def stacked_q(s):
        q = q_ref[SUB * s:SUB * (s + 1), :].astype(F32) * (HEAD_DIM ** -0.5)
        return jnp.concatenate([jnp.where(first, q, 0.0), jnp.where(first, 0.0, q)],
                               axis=0).astype(BF16)

    def cores(items):
        zs = [_dot_nt(qs, kb) for qs, kb, _, _ in items]
        sps = []
        for z, (_, _, _, visible) in zip(zs, items):
            sp = jnp.maximum(z, 0.0) + jnp.log(1.0 + jnp.exp2(jnp.abs(z) * (-LOG2E)))
            sps.append(sp if visible is None else jnp.where(visible, sp, 0.0))
        css = [jnp.dot(sp.astype(BF16), u[:sp.shape[1], :sp.shape[1]], preferred_element_type=F32)
               for sp in sps]
        out = []
        for z, sp, cs, (_, _, vb, visible) in zip(zs, sps, css, items):
            w = jnp.exp(z - sp - cs)
            if visible is not None:
                w = jnp.where(visible, w, 0.0)
            wcat = jnp.concatenate([w[:SUB], w[SUB:]], axis=1).astype(BF16)
            first_v = lax.broadcasted_iota(jnp.int32, vb.shape, 1) < HEAD_DIM
            vbd = jnp.concatenate([jnp.where(first_v, vb, 0), jnp.where(first_v, 0, vb)], axis=0)
            out.append((jnp.dot(wcat, vbd, preferred_element_type=F32), cs[:, 0:1] + sp[:, 0:1]))
        return out

    def nearer(carry):
        f = jnp.exp(-carry)
        return jnp.where(first, f[:SUB], f[SUB:])

    def keys(j):
        start = pl.multiple_of(j * SUB, SUB)
        return k_ref[pl.ds(start, SUB), :], v_ref[pl.ds(start, SUB), :]

    qss = [stacked_q(s) for s in range(n_sub)]
    items = []
    for s in range(n_sub):
        ii = i * n_sub + s
        items.append((qss[s],) + keys(ii) + (causal,))
        items.append((qss[s],) + keys(jnp.maximum(ii - 1, 0)) + (None,))
    first_two = cores(items)

    carries, accs = [], []
    for s in range(n_sub):
        (pv_d, tot_d), (pv_p, tot_p) = first_two[2 * s], first_two[2 * s + 1]
        has_prev = (i * n_sub + s > 0).astype(F32)
        accs.append(pv_d + pv_p * (nearer(tot_d) * has_prev))
        carries.append(tot_d + tot_p * has_prev)

    def alive(carry):
        return jnp.min(carry) <= DEAD_CARRY

    def farther(*operands):
        out = []
        for s in range(n_sub):
            carry, acc = operands[s], operands[n_sub + s]

            def body(state, s=s):
                j, _, carry, acc = state
                (pv, tot), = cores([(qss[s],) + keys(j) + (None,)])
                carry_new = carry + tot
                return j - 1, alive(carry_new), carry_new, acc + pv * nearer(carry)

            _, still, carry, acc = lax.while_loop(
                lambda state: (state[0] >= 0) & state[1], body,
                (i * n_sub + s - 2, alive(carry), carry, acc))

            def with_meta(carry, acc, s=s):
                (pv, _), = cores([(qss[s], km_ref[...], vm_ref[...], meta_visible)])
                return acc + pv * nearer(carry)

            out.append(lax.cond(still, with_meta, lambda carry, acc: acc, carry, acc))
        return tuple(out)

    any_alive = functools.reduce(jnp.logical_or, [alive(carry) for carry in carries])
    accs = lax.cond(any_alive, farther, lambda *operands: tuple(operands[n_sub:]), *carries, *accs)
    for s in range(n_sub):
        g = g_ref[SUB * s:SUB * (s + 1), :].astype(F32)
        o_ref[SUB * s:SUB * (s + 1), :] = (accs[s] * (g * _sigmoid(g))).astype(o_ref.dtype)


def _sb_attn(p_main, k_meta, v_meta, u, batch, seq, tq):
    nq = seq // tq

    def seq_map(seg):
        return lambda b, p, i: (b, seg * PAIRS + p)

    def tile_map(seg):
        return lambda b, p, i: (b * nq + i, seg * PAIRS + p)

    return pl.pallas_call(
        functools.partial(_sb_kernel, tq=tq),
        grid=(batch, PAIRS, nq),
        in_specs=[
            pl.BlockSpec((tq, LANES), tile_map(SEG_Q)),
            pl.BlockSpec((seq, LANES), seq_map(SEG_KS)),
            pl.BlockSpec((seq, LANES), seq_map(SEG_VS)),
            pl.BlockSpec((LANES, LANES), lambda b, p, i: (0, p)),
            pl.BlockSpec((LANES, LANES), lambda b, p, i: (0, p)),
            pl.BlockSpec((tq, LANES), tile_map(SEG_GS)),
            pl.BlockSpec((SUB, SUB), lambda b, p, i: (0, 0)),
        ],
        out_specs=pl.BlockSpec((tq, LANES), lambda b, p, i: (b * nq + i, p)),
        out_shape=jax.ShapeDtypeStruct((batch * seq, WIDTH), BF16),
        compiler_params=pltpu.CompilerParams(
            dimension_semantics=("arbitrary", "arbitrary", "arbitrary"),
            vmem_limit_bytes=VMEM_LIMIT),
        name="sb_attn",
    )(p_main, p_main, p_main, k_meta, v_meta, p_main, u)


def _out_kernel(yr_ref, ys_ref, mr_ref, ms_ref, x_ref, wr_ref, ws_ref, wo_ref, g_ref, o_ref):
    pr = jnp.dot(yr_ref[...], wr_ref[...], preferred_element_type=F32)
    ps = jnp.dot(ys_ref[...], ws_ref[...], preferred_element_type=F32)
    mixed = (_sigmoid(mr_ref[...].astype(F32)) * pr + _sigmoid(ms_ref[...].astype(F32)) * ps)
    o = jnp.dot(mixed.astype(BF16), wo_ref[...], preferred_element_type=F32)
    inv = lax.rsqrt(jnp.mean(o * o, axis=-1, keepdims=True) + RMS_EPS)
    o_ref[...] = x_ref[...] + o * inv * g_ref[...]


def _out_proj(y_rwkv, y_sb, p_main, x2d, w_r, w_s, w_o, g, tm):
    rows = x2d.shape[0]
    w_spec = pl.BlockSpec((WIDTH, D_MODEL), lambda i: (0, 0), pipeline_mode=pl.Buffered(1))
    return pl.pallas_call(
        _out_kernel,
        grid=(rows // tm,),
        in_specs=[
            pl.BlockSpec((tm, WIDTH), lambda i: (i, 0)),
            pl.BlockSpec((tm, WIDTH), lambda i: (i, 0)),
            pl.BlockSpec((tm, WIDTH), lambda i: (i, SEG_MR)),
            pl.BlockSpec((tm, WIDTH), lambda i: (i, SEG_MS)),
            pl.BlockSpec((tm, D_MODEL), lambda i: (i, 0)),
            w_spec, w_spec, w_spec,
            pl.BlockSpec((1, D_MODEL), lambda i: (0, 0)),
        ],
        out_specs=pl.BlockSpec((tm, D_MODEL), lambda i: (i, 0)),
        out_shape=jax.ShapeDtypeStruct((rows, D_MODEL), F32),
        compiler_params=pltpu.CompilerParams(
            dimension_semantics=("arbitrary",), vmem_limit_bytes=VMEM_LIMIT),
        name="out_proj",
    )(y_rwkv, y_sb, p_main, p_main, x2d, w_r, w_s, w_o, g)


def _tiles(batch, seq):
    rows = batch * seq
    tm_in = min(1024, rows)
    tm_out = min(1024, rows)
    tq = min(2048, seq)
    return tm_in, 2560, tm_out, tq


def kernel(x, meta_tokens, pre_norm_g, post_norm_g, w_in, rwkv_mu, rwkv_w0, rwkv_w_up, rwkv_a0,
           rwkv_a_up, rwkv_k_k, rwkv_k_a, rwkv_r_k, rwkv_gn_g, rwkv_gn_b, w_proj_rwkv, w_proj_sb,
           w_out):
    batch, seq, d_model = x.shape
    assert d_model == D_MODEL and w_in.shape[0] == 1, "single-layer kernel"
    assert seq % 512 == 0 and meta_tokens.shape == (N_META, D_MODEL)
    tm_in, tn_in, tm_out, tq = _tiles(batch, seq)

    w = w_in[0].astype(BF16)
    w_main = jnp.concatenate([w[:, :3 * WIDTH], w[:, 3 * WIDTH + 2 * LORA:]], axis=1)
    w_lo = w[:, 3 * WIDTH:3 * WIDTH + 2 * LORA]
    g_pre = pre_norm_g[0][None, :]

    x2d = x.reshape(batch * seq, D_MODEL)
    p_main, lora_main = _in_proj(x2d, g_pre, w_main, w_lo, tm_in, tn_in)
    meta_rows = jnp.zeros((CHUNK, D_MODEL), F32).at[CHUNK - N_META:].set(meta_tokens.astype(F32))
    p_meta, lora_meta = _in_proj(meta_rows, g_pre, w_main, w_lo, CHUNK, tn_in)

    mu = rwkv_mu[0]
    mu3 = mu[:3 * WIDTH].reshape(3, WIDTH)
    mu_lora = mu[3 * WIDTH:][None, :]
    as_row = lambda t: t[0].reshape(1, WIDTH)
    rows_vec = (mu3, as_row(rwkv_w0), as_row(rwkv_a0), as_row(rwkv_k_k), as_row(rwkv_k_a),
                as_row(rwkv_r_k), as_row(rwkv_gn_g), as_row(rwkv_gn_b))
    w_lora_up = jnp.concatenate([rwkv_w_up[0], rwkv_a_up[0]], axis=0).astype(BF16)
    idx = jnp.arange(256)
    e256 = (idx[:, None] // HEAD_DIM == idx[None, :] // HEAD_DIM).astype(BF16)
    zero_start = (jnp.zeros((PAIRS, LANES, LANES), F32), jnp.zeros((8, WIDTH), F32),
                  jnp.zeros((8, 2 * LORA), F32))
    _, *after_meta = _rwkv(p_meta, lora_meta, zero_start, rows_vec, mu_lora, w_lora_up, e256,
                           1, CHUNK, 1)
    y_rwkv, *_ = _rwkv(p_main, lora_main, after_meta, rows_vec, mu_lora, w_lora_up, e256,
                       batch, seq, RW_CHUNKS)

    kidx = jnp.arange(SUB)
    u_tri = (kidx[:, None] > kidx[None, :]).astype(BF16)
    pad = jnp.zeros((LANES - N_META, WIDTH), BF16)
    meta_seg = lambda seg: jnp.concatenate(
        [p_meta[CHUNK - N_META:, seg * WIDTH:(seg + 1) * WIDTH], pad], axis=0)
    y_sb = _sb_attn(p_main, meta_seg(SEG_KS), meta_seg(SEG_VS), u_tri, batch, seq, tq)

    out = _out_proj(y_rwkv, y_sb, p_main, x2d, w_proj_rwkv[0].astype(BF16),
                    w_proj_sb[0].astype(BF16), w_out[0].astype(BF16), post_norm_g[0][None, :],
                    tm_out)
    return out.reshape(batch, seq, D_MODEL)
```
